```python
import functools
import jax, jax.numpy as jnp
from jax import lax
import numpy as np

D_MODEL = 1024
BATCH = 8
SEQ = 2048
DEPTH = 1
DEC_BATCH = 128
DEC_SEQ = 8
PAST_LEN = 8192
PAGE_SIZE = 128

N_META = 16
SB_HEADS = 8
SB_HEAD_DIM = 64
SB_WIDTH = SB_HEADS * SB_HEAD_DIM
SB_BIAS_INIT = -6.0
CONV_CH = D_MODEL // 2
CONV_KERNEL = 31
FFN_KERNEL = 3
D_FF = 2816
Q_BLOCK = 128
RMS_EPS = 1e-6
LN_EPS = 1e-5
SPLITS = [SB_WIDTH, 2 * SB_WIDTH, 3 * SB_WIDTH, 3 * SB_WIDTH + CONV_CH,
          3 * SB_WIDTH + 2 * CONV_CH, 3 * SB_WIDTH + 2 * CONV_CH + D_MODEL]
IN_COLS = 3 * SB_WIDTH + 2 * CONV_CH + 2 * D_MODEL

kernel_name = 'stickbreak_conformer_hybrid_step'


def rms_norm(x, g):
    xf = x.astype(jnp.float32)
    y = xf * lax.rsqrt(jnp.mean(xf * xf, axis=-1, keepdims=True) + RMS_EPS)
    return (y * g.astype(jnp.float32)).astype(x.dtype)


def layer_norm(x, g, b):
    xf = x.astype(jnp.float32)
    mu = jnp.mean(xf, axis=-1, keepdims=True)
    var = jnp.mean(jnp.square(xf - mu), axis=-1, keepdims=True)
    y = (xf - mu) * lax.rsqrt(var + LN_EPS) * g.astype(jnp.float32) + b.astype(jnp.float32)
    return y.astype(x.dtype)


def causal_dwconv(x_hist, w, b):
    c = x_hist.shape[-1]
    y = lax.conv_general_dilated(x_hist, w[:, None, :].astype(x_hist.dtype), window_strides=(1,),
                                 padding='VALID', dimension_numbers=('NWC', 'WIO', 'NWC'),
                                 feature_group_count=c)
    return y + b.astype(y.dtype)


def stick_breaking(q, k, v, bias, q_pos, k_pos):
    z = jnp.einsum('nqhe,nkhe->nhqk', q.astype(jnp.float32), k.astype(jnp.float32)) * (SB_HEAD_DIM ** -0.5)
    z = z + bias.astype(jnp.float32)[None, :, None, None]
    mask = k_pos[None, :] < q_pos[:, None]
    log_stay = jnp.where(mask, jax.nn.log_sigmoid(-z), 0.0)
    later = lax.cumsum(log_stay, axis=3, reverse=True) - log_stay
    w = jnp.where(mask, jnp.exp(jax.nn.log_sigmoid(z) + later), 0.0)
    return jnp.einsum('nhqk,nkhe->nqhe', w.astype(v.dtype), v)


def prompt_attention(q, k, v, bias):
    n, length = q.shape[0], q.shape[1]
    nb = -(-length // Q_BLOCK)
    lp = nb * Q_BLOCK
    qb = jnp.pad(q, ((0, 0), (0, lp - length), (0, 0), (0, 0)))
    qb = qb.reshape(n, nb, Q_BLOCK, SB_HEADS, SB_HEAD_DIM).transpose(1, 0, 2, 3, 4)
    q_pos = jnp.arange(lp, dtype=jnp.int32).reshape(nb, Q_BLOCK)
    k_pos = jnp.arange(length, dtype=jnp.int32)
    out = lax.map(lambda a: stick_breaking(a[0], k, v, bias, a[1], k_pos), (qb, q_pos))
    return out.transpose(1, 0, 2, 3, 4).reshape(n, lp, SB_HEADS, SB_HEAD_DIM)[:, :length]


def sample_attention(q, k_new, v_new, bias, cache_k, cache_v, page_table, layer):
    n_past = page_table.shape[1] * PAGE_SIZE
    s = q.shape[1]
    q_pos = n_past + jnp.arange(s, dtype=jnp.int32)
    k_pos = jnp.arange(n_past + s, dtype=jnp.int32)
    pool_k = cache_k[layer]
    pool_v = cache_v[layer]

    def one(args):
        qi, ki, vi, pages = args
        kp = jnp.take(pool_k, pages, axis=0).reshape(n_past, SB_HEADS, SB_HEAD_DIM).astype(ki.dtype)
        vp = jnp.take(pool_v, pages, axis=0).reshape(n_past, SB_HEADS, SB_HEAD_DIM).astype(vi.dtype)
        kk = jnp.concatenate([kp, ki], axis=0)[None]
        vv = jnp.concatenate([vp, vi], axis=0)[None]
        return stick_breaking(qi[None], kk, vv, bias, q_pos, k_pos)[0]

    return lax.map(one, (q, k_new, v_new, page_table))


def layer_forward(x, attend, conv_hist, ffn_hist, norm_mix_pre, norm_mix_post, w_in, sb_bias,
                  conv_dw_w, conv_dw_b, conv_ln_g, conv_ln_b, w_conv_out, w_attn_out, w_mix_out,
                  norm_ffn_pre, norm_ffn_post, w_ffn_in, ffn_dw_w, ffn_dw_b, w_ffn_out):
    n, t, _ = x.shape
    h = rms_norm(x, norm_mix_pre)
    q, k, v, glu_a, glu_b, gate_att, gate_conv = jnp.split(h @ w_in, SPLITS, axis=-1)
    q = q.reshape(n, t, SB_HEADS, SB_HEAD_DIM)
    k = k.reshape(n, t, SB_HEADS, SB_HEAD_DIM)
    v = v.reshape(n, t, SB_HEADS, SB_HEAD_DIM)
    att = attend(q, k, v, sb_bias).reshape(n, t, SB_WIDTH) @ w_attn_out
    u = glu_a * jax.nn.sigmoid(glu_b)
    u_hist = jnp.concatenate([conv_hist.astype(u.dtype), u], axis=1)
    c = causal_dwconv(u_hist, conv_dw_w, conv_dw_b)
    c = jax.nn.silu(layer_norm(c, conv_ln_g, conv_ln_b)) @ w_conv_out
    merged = jax.nn.sigmoid(gate_att) * att + jax.nn.sigmoid(gate_conv) * c
    x = x + rms_norm(merged @ w_mix_out, norm_mix_post)
    h = rms_norm(x, norm_ffn_pre)
    up, gate = jnp.split(h @ w_ffn_in, 2, axis=-1)
    g_hist = jnp.concatenate([ffn_hist.astype(gate.dtype), gate], axis=1)
    g = causal_dwconv(g_hist, ffn_dw_w, ffn_dw_b)
    x = x + rms_norm((jax.nn.gelu(g, approximate=True) * up) @ w_ffn_out, norm_ffn_post)
    return x, k, v, u_hist[:, -(CONV_KERNEL - 1):], g_hist[:, -(FFN_KERNEL - 1):]


def setup_inputs(seed: int = 0) -> dict:
    key = jax.random.key(seed)
    ks = jax.random.split(key, 32)
    n_pages = PAST_LEN // PAGE_SIZE
    n_used = DEC_BATCH * n_pages
    n_pool = (5 * n_used + 3) // 4
    f32 = jnp.float32

    def nrm(k, shape, scale=1.0):
        return jax.random.normal(k, shape, f32) * scale

    perm = jax.random.permutation(ks[0], n_pool)
    page_table = perm[:n_used].reshape(DEC_BATCH, n_pages).astype(jnp.int32)
    return {
        'x_prompt': nrm(ks[1], (BATCH, SEQ, D_MODEL)),
        'x_sample': nrm(ks[2], (DEC_BATCH, DEC_SEQ, D_MODEL)),
        'cache_k': nrm(ks[3], (DEPTH, n_pool, PAGE_SIZE, SB_HEADS, SB_HEAD_DIM)),
        'cache_v': nrm(ks[4], (DEPTH, n_pool, PAGE_SIZE, SB_HEADS, SB_HEAD_DIM)),
        'state_conv': nrm(ks[5], (DEPTH, DEC_BATCH, CONV_KERNEL - 1, CONV_CH), 0.5),
        'state_ffn': nrm(ks[6], (DEPTH, DEC_BATCH, FFN_KERNEL - 1, D_FF), 0.5),
        'page_table': page_table,
        'meta_tokens': nrm(ks[7], (N_META, D_MODEL)),
        'norm_mix_pre': 1.0 + nrm(ks[8], (DEPTH, D_MODEL), 0.05),
        'norm_mix_post': 1.0 + nrm(ks[9], (DEPTH, D_MODEL), 0.05),
        'w_in': nrm(ks[10], (DEPTH, D_MODEL, IN_COLS), D_MODEL ** -0.5),
        'sb_bias': SB_BIAS_INIT + nrm(ks[24], (DEPTH, SB_HEADS), 0.1),
        'conv_dw_w': nrm(ks[11], (DEPTH, CONV_KERNEL, CONV_CH), CONV_KERNEL ** -0.5),
        'conv_dw_b': nrm(ks[12], (DEPTH, CONV_CH), 0.02),
        'conv_ln_g': 1.0 + nrm(ks[13], (DEPTH, CONV_CH), 0.05),
        'conv_ln_b': nrm(ks[14], (DEPTH, CONV_CH), 0.02),
        'w_conv_out': nrm(ks[15], (DEPTH, CONV_CH, D_MODEL), CONV_CH ** -0.5),
        'w_attn_out': nrm(ks[16], (DEPTH, SB_WIDTH, D_MODEL), SB_WIDTH ** -0.5),
        'w_mix_out': nrm(ks[17], (DEPTH, D_MODEL, D_MODEL), D_MODEL ** -0.5),
        'norm_ffn_pre': 1.0 + nrm(ks[18], (DEPTH, D_MODEL), 0.05),
        'norm_ffn_post': 1.0 + nrm(ks[19], (DEPTH, D_MODEL), 0.05),
        'w_ffn_in': nrm(ks[20], (DEPTH, D_MODEL, 2 * D_FF), D_MODEL ** -0.5),
        'ffn_dw_w': nrm(ks[21], (DEPTH, FFN_KERNEL, D_FF), FFN_KERNEL ** -0.5),
        'ffn_dw_b': nrm(ks[22], (DEPTH, D_FF), 0.02),
        'w_ffn_out': nrm(ks[23], (DEPTH, D_FF, D_MODEL), D_FF ** -0.5),
    }


def reference(x_prompt, x_sample, cache_k, cache_v, state_conv, state_ffn, page_table, meta_tokens,
              norm_mix_pre, norm_mix_post, w_in, sb_bias, conv_dw_w, conv_dw_b, conv_ln_g, conv_ln_b,
              w_conv_out, w_attn_out, w_mix_out, norm_ffn_pre, norm_ffn_post, w_ffn_in, ffn_dw_w,
              ffn_dw_b, w_ffn_out):
    b = x_prompt.shape[0]
    meta = jnp.broadcast_to(meta_tokens.astype(x_prompt.dtype)[None], (b, N_META, x_prompt.shape[-1]))
    xp = jnp.concatenate([meta, x_prompt], axis=1)
    xs = x_sample
    conv0 = jnp.zeros((b, CONV_KERNEL - 1, CONV_CH), xp.dtype)
    ffn0 = jnp.zeros((b, FFN_KERNEL - 1, D_FF), xp.dtype)
    kp_l, vp_l, cp_l, fp_l, ks_l, vs_l, cs_l, fs_l = [], [], [], [], [], [], [], []
    for l in range(DEPTH):
        lw = (norm_mix_pre[l], norm_mix_post[l], w_in[l], sb_bias[l], conv_dw_w[l], conv_dw_b[l],
              conv_ln_g[l], conv_ln_b[l], w_conv_out[l], w_attn_out[l], w_mix_out[l], norm_ffn_pre[l],
              norm_ffn_post[l], w_ffn_in[l], ffn_dw_w[l], ffn_dw_b[l], w_ffn_out[l])
        xp, kp, vp, cp, fp = layer_forward(xp, prompt_attention, conv0, ffn0, *lw)
        attend_s = functools.partial(sample_attention, cache_k=cache_k, cache_v=cache_v,
                                     page_table=page_table, layer=l)
        xs, ksn, vsn, csn, fsn = layer_forward(xs, attend_s, state_conv[l], state_ffn[l], *lw)
        kp_l.append(kp); vp_l.append(vp); cp_l.append(cp); fp_l.append(fp)
        ks_l.append(ksn); vs_l.append(vsn); cs_l.append(csn); fs_l.append(fsn)
    return (xp[:, N_META:], xs, jnp.stack(kp_l), jnp.stack(vp_l), jnp.stack(cp_l), jnp.stack(fp_l),
            jnp.stack(ks_l), jnp.stack(vs_l), jnp.stack(cs_l), jnp.stack(fs_l))
```

```python
import functools

import jax
import jax.numpy as jnp
from jax import lax
from jax.experimental import pallas as pl
from jax.experimental.pallas import tpu as pltpu

F32 = jnp.float32
BF16 = jnp.bfloat16

D_MODEL = 1024
N_META = 16
HEADS = 8
HEAD_DIM = 64
SB_WIDTH = HEADS * HEAD_DIM
CONV_CH = 512
CONV_KERNEL = 31
FFN_KERNEL = 3
D_FF = 2816
PAGE = 128
RMS_EPS = 1e-6
LN_EPS = 1e-5

LANES = 128
HEAD_TILE = 128
HEAD_PAD = HEAD_TILE - N_META
CONV_HALO = 32
FFN_HALO = 8
VMEM_LIMIT = 56 * 1024 * 1024

ATT_TQ = 256
ATT_TK = 256
PAGES_PER_STEP = 8


def _params(*sem):
    return pltpu.CompilerParams(dimension_semantics=sem, vmem_limit_bytes=VMEM_LIMIT)


def _const_spec(shape):
    zeros = (0,) * len(shape)
    return pl.BlockSpec(shape, lambda *_: zeros)


def _rms(x, g):
    return x * lax.rsqrt(jnp.mean(x * x, axis=-1, keepdims=True) + RMS_EPS) * g


def _inproj_kernel(x_ref, g_ref, w_ref, q_ref, k_ref, v_ref, kb_ref, vb_ref, u_ref, ga_ref, gc_ref):
    h = _rms(x_ref[...], g_ref[...]).astype(BF16)

    def proj(lo, hi):
        return jnp.dot(h, w_ref[:, lo:hi], preferred_element_type=F32)

    w = SB_WIDTH
    q_ref[...] = (proj(0, w) * (HEAD_DIM ** -0.5)).astype(BF16)
    k = proj(w, 2 * w)
    v = proj(2 * w, 3 * w)
    k_ref[...] = k
    v_ref[...] = v
    kb_ref[...] = k.astype(BF16)
    vb_ref[...] = v.astype(BF16)
    c0 = 3 * w
    u_ref[...] = proj(c0, c0 + CONV_CH) * jax.nn.sigmoid(proj(c0 + CONV_CH, c0 + 2 * CONV_CH))
    c1 = c0 + 2 * CONV_CH
    ga_ref[...] = jax.nn.sigmoid(proj(c1, c1 + D_MODEL))
    gc_ref[...] = jax.nn.sigmoid(proj(c1 + D_MODEL, c1 + 2 * D_MODEL))


def _inproj(x, g, w_bf, tm):
    rows = x.shape[0]
    ncol = w_bf.shape[1]
    row = lambda width: pl.BlockSpec((tm, width), lambda i: (i, 0))
    shp = lambda width, dt: jax.ShapeDtypeStruct((rows, width), dt)
    return pl.pallas_call(
        _inproj_kernel,
        grid=(rows // tm,),
        in_specs=[row(D_MODEL), _const_spec((1, D_MODEL)), _const_spec((D_MODEL, ncol))],
        out_specs=[row(SB_WIDTH)] * 5 + [row(CONV_CH), row(D_MODEL), row(D_MODEL)],
        out_shape=[shp(SB_WIDTH, BF16), shp(SB_WIDTH, F32), shp(SB_WIDTH, F32), shp(SB_WIDTH, BF16),
                   shp(SB_WIDTH, BF16), shp(CONV_CH, F32), shp(D_MODEL, F32), shp(D_MODEL, F32)],
        compiler_params=_params("parallel"),
        name="inproj",
    )(x, g, w_bf)


def _suffix_matrix(tk):
    s = lax.broadcasted_iota(jnp.int32, (tk, tk), 0)
    j = lax.broadcasted_iota(jnp.int32, (tk, tk), 1)
    u = (s > j).astype(BF16)
    return jnp.concatenate([u, u], axis=0)


def _sb_block(z, mask, u2, carry):
    sp = jnp.log(1.0 + jnp.exp(-jnp.abs(z)))
    log_stay = -(jnp.maximum(z, 0.0) + sp)
    log_beta = z + log_stay
    if mask is not None:
        log_stay = jnp.where(mask, log_stay, 0.0)
    hi = log_stay.astype(BF16)
    lo = (log_stay - hi.astype(F32)).astype(BF16)
    later = jnp.dot(jnp.concatenate([hi, lo], axis=1), u2, preferred_element_type=F32) + carry
    w = jnp.exp(log_beta + later)
    if mask is not None:
        w = jnp.where(mask, w, 0.0)
    new_carry = later[:, 0:1] + log_stay[:, 0:1]
    return w.astype(BF16), new_carry


def _qk(q, k):
    return lax.dot_general(q, k, (((1,), (1,)), ((), ())), preferred_element_type=F32)


def _head_key_mask(tq):
    col = lax.broadcasted_iota(jnp.int32, (tq, HEAD_TILE), 1)
    return col >= HEAD_PAD


def _prompt_attn_kernel(bias_ref, q_ref, k_ref, v_ref, kh_ref, vh_ref, u2_ref, u2h_ref, o_ref,
                        acc_ref, carry_ref):
    hp = pl.program_id(1)
    i = pl.program_id(2)
    tq, tk = ATT_TQ, ATT_TK
    lane = lax.broadcasted_iota(jnp.int32, (1, LANES), 1)
    row = lax.broadcasted_iota(jnp.int32, (tq, tk), 0)
    col = lax.broadcasted_iota(jnp.int32, (tq, tk), 1)
    causal = col < row
    outs = []
    for s in range(2):
        bias = bias_ref[0, 2 * hp + s]
        head_lanes = (lane // HEAD_DIM) == s
        qm = jnp.where(head_lanes, q_ref[...], jnp.zeros_like(q_ref[...]))

        def step(kblk, vblk, mask, u2, carry):
            w, carry = _sb_block(_qk(qm, kblk) + bias, mask, u2, carry)
            return jnp.dot(w, vblk, preferred_element_type=F32), carry

        start = pl.multiple_of(i * tk, tk)
        pv, carry = step(k_ref[pl.ds(start, tk), :], v_ref[pl.ds(start, tk), :], causal, u2_ref[...],
                         jnp.zeros((tq, 1), F32))
        acc_ref[...] = pv
        carry_ref[...] = carry

        def body(j, _):
            st = pl.multiple_of((i - 1 - j) * tk, tk)
            pv, carry = step(k_ref[pl.ds(st, tk), :], v_ref[pl.ds(st, tk), :], None, u2_ref[...],
                             carry_ref[...])
            acc_ref[...] += pv
            carry_ref[...] = carry
            return 0

        lax.fori_loop(0, i, body, 0)
        pv, _ = step(kh_ref[...], vh_ref[...], _head_key_mask(tq), u2h_ref[...], carry_ref[...])
        outs.append((head_lanes, acc_ref[...] + pv))
    o_ref[...] = jnp.where(outs[0][0], outs[0][1], outs[1][1]).astype(o_ref.dtype)


def _prompt_attn(bias, q, kb, vb, kh, vh, batch, seq):
    nq = seq // ATT_TQ
    pair = lambda rows, imap: pl.BlockSpec((rows, LANES), imap)
    return pl.pallas_call(
        _prompt_attn_kernel,
        grid=(batch, HEADS // 2, nq),
        in_specs=[pl.BlockSpec(memory_space=pltpu.SMEM),
                  pair(ATT_TQ, lambda b, hp, i: (b * nq + i, hp)),
                  pair(seq, lambda b, hp, i: (b, hp)),
                  pair(seq, lambda b, hp, i: (b, hp)),
                  pair(HEAD_TILE, lambda b, hp, i: (0, hp)),
                  pair(HEAD_TILE, lambda b, hp, i: (0, hp)),
                  _const_spec((2 * ATT_TK, ATT_TK)),
                  _const_spec((2 * HEAD_TILE, HEAD_TILE))],
        out_specs=pair(ATT_TQ, lambda b, hp, i: (b * nq + i, hp)),
        out_shape=jax.ShapeDtypeStruct((batch * seq, SB_WIDTH), BF16),
        scratch_shapes=[pltpu.VMEM((ATT_TQ, LANES), F32), pltpu.VMEM((ATT_TQ, 1), F32)],
        compiler_params=_params("parallel", "parallel", "parallel"),
        name="prompt_attn",
    )(bias, q, kb, vb, kh, vh, _suffix_matrix(ATT_TK), _suffix_matrix(HEAD_TILE))


def _head_attn_kernel(bias_ref, q_ref, k_ref, v_ref, u2_ref, o_ref):
    hp = pl.program_id(0)
    t = HEAD_TILE
    lane = lax.broadcasted_iota(jnp.int32, (1, LANES), 1)
    row = lax.broadcasted_iota(jnp.int32, (t, t), 0)
    col = lax.broadcasted_iota(jnp.int32, (t, t), 1)
    mask = (col < row) & (col >= HEAD_PAD)
    outs = []
    for s in range(2):
        head_lanes = (lane // HEAD_DIM) == s
        qm = jnp.where(head_lanes, q_ref[...], jnp.zeros_like(q_ref[...]))
        z = _qk(qm, k_ref[...]) + bias_ref[0, 2 * hp + s]
        w, _ = _sb_block(z, mask, u2_ref[...], jnp.zeros((t, 1), F32))
        outs.append((head_lanes, jnp.dot(w, v_ref[...], preferred_element_type=F32)))
    o_ref[...] = jnp.where(outs[0][0], outs[0][1], outs[1][1]).astype(o_ref.dtype)


def _head_attn(bias, q, kb, vb):
    pair = pl.BlockSpec((HEAD_TILE, LANES), lambda hp: (0, hp))
    return pl.pallas_call(
        _head_attn_kernel,
        grid=(HEADS // 2,),
        in_specs=[pl.BlockSpec(memory_space=pltpu.SMEM), pair, pair, pair,
                  _const_spec((2 * HEAD_TILE, HEAD_TILE))],
        out_specs=pair,
        out_shape=jax.ShapeDtypeStruct((HEAD_TILE, SB_WIDTH), BF16),
        compiler_params=_params("parallel"),
        name="head_attn",
    )(bias, q, kb, vb, _suffix_matrix(HEAD_TILE))


def _sample_attn_kernel(pt_ref, bias_ref, q_ref, kn_ref, vn_ref, *rest):
    npg = PAGES_PER_STEP
    k_refs = rest[:npg]
    v_refs = rest[npg:2 * npg]
    u2_ref, o_ref, qb_ref, bias_rows_ref, acc_ref, carry_ref, new_ref = rest[2 * npg:]
    j = pl.program_id(1)
    nq = q_ref.shape[0]
    rows = HEADS * nq
    row_head = lax.broadcasted_iota(jnp.int32, (rows, SB_WIDTH), 0) // nq
    lane_head = lax.broadcasted_iota(jnp.int32, (rows, SB_WIDTH), 1) // HEAD_DIM

    def weights(z, mask):
        w, carry = _sb_block(z + bias_rows_ref[...], mask, u2_ref[...], carry_ref[...])
        carry_ref[...] = carry
        return w

    def new_block(kblk, vblk, mask):
        w = weights(_qk(qb_ref[...], kblk.astype(BF16)), mask)
        acc_ref[...] += jnp.dot(w, vblk.astype(BF16), preferred_element_type=F32)

    def page_block(kt_ref, vt_ref):
        kt = kt_ref[...].reshape(SB_WIDTH, PAGE).astype(BF16)
        w = weights(jnp.dot(qb_ref[...], kt, preferred_element_type=F32), None)
        vt = vt_ref[...].reshape(SB_WIDTH, PAGE).astype(BF16)
        acc_ref[...] += _qk(w, vt)

    @pl.when(j == 0)
    def _():
        q = q_ref[...].astype(BF16)
        qrep = jnp.concatenate([q] * HEADS, axis=0)
        qb_ref[...] = jnp.where(row_head == lane_head, qrep, jnp.zeros_like(qrep))
        bias_rows_ref[...] = jnp.concatenate(
            [jnp.full((nq, 1), bias_ref[0, h], F32) for h in range(HEADS)], axis=0)
        acc_ref[...] = jnp.zeros_like(acc_ref)
        carry_ref[...] = jnp.zeros_like(carry_ref)
        new_ref[...] = jnp.zeros_like(new_ref)
        new_ref[0, 0:nq, :] = kn_ref[...]
        new_ref[1, 0:nq, :] = vn_ref[...]
        qi = lax.broadcasted_iota(jnp.int32, (rows, PAGE), 0) % nq
        ki = lax.broadcasted_iota(jnp.int32, (rows, PAGE), 1)
        new_block(new_ref[0], new_ref[1], ki < qi)

    for r in range(npg):
        page_block(k_refs[r], v_refs[r])

    @pl.when(j == pl.num_programs(1) - 1)
    def _():
        out = jnp.zeros((nq, SB_WIDTH), F32)
        for h in range(HEADS):
            out = out + jnp.where(lane_head[0:nq] == h, acc_ref[h * nq:(h + 1) * nq, :], 0.0)
        o_ref[...] = out


def _sample_attn(page_table, bias, q, k_new, v_new, pool_k, pool_v):
    nseq, nq, _ = q.shape
    npages = page_table.shape[1]
    npg = PAGES_PER_STEP
    nsteps = npages // npg
    rows = HEADS * nq
    seq_spec = pl.BlockSpec((None, nq, SB_WIDTH), lambda s, j, pt: (s, 0, 0))

    def page_spec(r):
        def imap(s, j, pt):
            return (pt[s * npages + (npages - 1 - (j * npg + r))], 0, 0, 0)
        return pl.BlockSpec((None, HEADS, HEAD_DIM, PAGE), imap)

    grid_spec = pltpu.PrefetchScalarGridSpec(
        num_scalar_prefetch=1,
        grid=(nseq, nsteps),
        in_specs=[pl.BlockSpec(memory_space=pltpu.SMEM), seq_spec, seq_spec, seq_spec]
                 + [page_spec(r) for r in range(npg)] * 2
                 + [pl.BlockSpec((2 * PAGE, PAGE), lambda s, j, pt: (0, 0))],
        out_specs=seq_spec,
        scratch_shapes=[pltpu.VMEM((rows, SB_WIDTH), BF16), pltpu.VMEM((rows, 1), F32),
                        pltpu.VMEM((rows, SB_WIDTH), F32), pltpu.VMEM((rows, 1), F32),
                        pltpu.VMEM((2, PAGE, SB_WIDTH), F32)],
    )
    return pl.pallas_call(
        _sample_attn_kernel,
        grid_spec=grid_spec,
        out_shape=jax.ShapeDtypeStruct((nseq, nq, SB_WIDTH), F32),
        compiler_params=_params("parallel", "arbitrary"),
        name="sample_attn",
    )(page_table.reshape(-1), bias, q, k_new, v_new, *([pool_k] * npg), *([pool_v] * npg),
      _suffix_matrix(PAGE))


def _ln_swish(c, g, b):
    mu = jnp.mean(c, axis=-1, keepdims=True)
    d = c - mu
    var = jnp.mean(d * d, axis=-1, keepdims=True)
    y = d * lax.rsqrt(var + LN_EPS) * g + b
    return y * jax.nn.sigmoid(y)


CONV_ROWS = 32


def _conv_seq_kernel(u_ref, prev_ref, first_ref, w_ref, b_ref, g_ref, beta_ref, o_ref, hist_ref):
    tc = u_ref.shape[0]
    first = pl.program_id(1) == 0
    hist_ref[0:CONV_HALO, :] = jnp.where(first, first_ref[...], prev_ref[...])
    hist_ref[CONV_HALO:CONV_HALO + tc, :] = u_ref[...]
    base = CONV_HALO - (CONV_KERNEL - 1)
    for r0 in range(0, tc, CONV_ROWS):
        acc = jnp.broadcast_to(b_ref[...], (CONV_ROWS, CONV_CH))
        for k in range(CONV_KERNEL):
            acc = acc + w_ref[k:k + 1, :] * hist_ref[pl.ds(base + r0 + k, CONV_ROWS), :]
        o_ref[r0:r0 + CONV_ROWS, :] = _ln_swish(acc, g_ref[...], beta_ref[...]).astype(o_ref.dtype)


def _conv_seq(u, first_hist, w, b, g, beta, nseq, tc):
    rows = u.shape[0]
    nt = rows // (nseq * tc)
    per = tc // CONV_HALO
    vec = _const_spec((1, CONV_CH))
    return pl.pallas_call(
        _conv_seq_kernel,
        grid=(nseq, nt),
        in_specs=[pl.BlockSpec((tc, CONV_CH), lambda s, i: (s * nt + i, 0)),
                  pl.BlockSpec((CONV_HALO, CONV_CH),
                               lambda s, i: (jnp.maximum((s * nt + i) * per - 1, 0), 0)),
                  _const_spec((CONV_HALO, CONV_CH)), _const_spec((CONV_KERNEL, CONV_CH)), vec, vec, vec],
        out_specs=pl.BlockSpec((tc, CONV_CH), lambda s, i: (s * nt + i, 0)),
        out_shape=jax.ShapeDtypeStruct((rows, CONV_CH), BF16),
        scratch_shapes=[pltpu.VMEM((CONV_HALO + tc, CONV_CH), F32)],
        compiler_params=_params("parallel", "parallel"),
        name="conv_seq",
    )(u, u, first_hist, w, b, g, beta)


def _conv_step_kernel(u_ref, st_ref, w_ref, b_ref, g_ref, beta_ref, o_ref, hist_ref):
    nq, grp, _ = u_ref.shape
    nh = st_ref.shape[0]
    hist_ref[0:nh] = st_ref[...]
    hist_ref[nh:nh + nq] = u_ref[...]
    for t in range(nq):
        acc = jnp.broadcast_to(b_ref[...], (grp, CONV_CH))
        for k in range(CONV_KERNEL):
            acc = acc + w_ref[k:k + 1, :] * hist_ref[t + k]
        o_ref[t] = _ln_swish(acc, g_ref[...], beta_ref[...]).astype(o_ref.dtype)


def _conv_step(u, state, w, b, g, beta, group):
    nq, nseq, _ = u.shape
    nh = state.shape[0]
    vec = _const_spec((1, CONV_CH))
    blk = lambda t: pl.BlockSpec((t, group, CONV_CH), lambda i: (0, i, 0))
    return pl.pallas_call(
        _conv_step_kernel,
        grid=(nseq // group,),
        in_specs=[blk(nq), blk(nh), _const_spec((CONV_KERNEL, CONV_CH)), vec, vec, vec],
        out_specs=blk(nq),
        out_shape=jax.ShapeDtypeStruct((nq, nseq, CONV_CH), BF16),
        scratch_shapes=[pltpu.VMEM((nh + nq, group, CONV_CH), F32)],
        compiler_params=_params("parallel"),
        name="conv_step",
    )(u, state, w, b, g, beta)


def _mix_kernel(x_ref, att_ref, c_ref, ga_ref, gc_ref, wa_ref, wc_ref, wm_ref, g_ref, o_ref):
    a = jnp.dot(att_ref[...], wa_ref[...], preferred_element_type=F32)
    c = jnp.dot(c_ref[...], wc_ref[...], preferred_element_type=F32)
    merged = (ga_ref[...] * a + gc_ref[...] * c).astype(BF16)
    m = jnp.dot(merged, wm_ref[...], preferred_element_type=F32)
    o_ref[...] = x_ref[...] + _rms(m, g_ref[...])


def _mix(x, att, cact, ga, gc, wa, wc, wm, g, tm):
    rows = x.shape[0]
    row = lambda width: pl.BlockSpec((tm, width), lambda i: (i, 0))
    return pl.pallas_call(
        _mix_kernel,
        grid=(rows // tm,),
        in_specs=[row(D_MODEL), row(SB_WIDTH), row(CONV_CH), row(D_MODEL), row(D_MODEL),
                  _const_spec((SB_WIDTH, D_MODEL)), _const_spec((CONV_CH, D_MODEL)),
                  _const_spec((D_MODEL, D_MODEL)), _const_spec((1, D_MODEL))],
        out_specs=row(D_MODEL),
        out_shape=jax.ShapeDtypeStruct((rows, D_MODEL), F32),
        compiler_params=_params("parallel"),
        name="mix",
    )(x, att, cact, ga, gc, wa, wc, wm, g)


def _ffn_tail(x, up, g, wout_ref, gpost_ref):
    act = (jax.nn.gelu(g, approximate=True) * up).astype(BF16)
    o = jnp.dot(act, wout_ref[...], preferred_element_type=F32)
    return x + _rms(o, gpost_ref[...])


def _ffn_seq_kernel(x_ref, ginit_ref, gpre_ref, win_ref, dww_ref, dwb_ref, wout_ref, gpost_ref,
                    y_ref, gtail_ref, gbuf_ref):
    tm = x_ref.shape[0]

    @pl.when(pl.program_id(1) == 0)
    def _():
        gbuf_ref[0:FFN_HALO, :] = ginit_ref[...]

    x = x_ref[...]
    h = _rms(x, gpre_ref[...]).astype(BF16)
    up = jnp.dot(h, win_ref[:, 0:D_FF], preferred_element_type=F32)
    gate = jnp.dot(h, win_ref[:, D_FF:2 * D_FF], preferred_element_type=F32)
    gbuf_ref[FFN_HALO:FFN_HALO + tm, :] = gate
    g = (dww_ref[0:1, :] * gbuf_ref[pl.ds(FFN_HALO - 2, tm), :]
         + dww_ref[1:2, :] * gbuf_ref[pl.ds(FFN_HALO - 1, tm), :]
         + dww_ref[2:3, :] * gate + dwb_ref[...])
    y_ref[...] = _ffn_tail(x, up, g, wout_ref, gpost_ref)
    tail = gbuf_ref[pl.ds(tm, FFN_HALO), :]
    gbuf_ref[0:FFN_HALO, :] = tail
    gtail_ref[...] = tail


def _ffn_seq(x, ginit, gpre, win, dww, dwb, wout, gpost, nseq, tm):
    rows = x.shape[0]
    nt = rows // (nseq * tm)
    return pl.pallas_call(
        _ffn_seq_kernel,
        grid=(nseq, nt),
        in_specs=[pl.BlockSpec((tm, D_MODEL), lambda s, i: (s * nt + i, 0)),
                  _const_spec((FFN_HALO, D_FF)), _const_spec((1, D_MODEL)),
                  _const_spec((D_MODEL, 2 * D_FF)), _const_spec((FFN_KERNEL, D_FF)),
                  _const_spec((1, D_FF)), _const_spec((D_FF, D_MODEL)), _const_spec((1, D_MODEL))],
        out_specs=[pl.BlockSpec((tm, D_MODEL), lambda s, i: (s * nt + i, 0)),
                   pl.BlockSpec((FFN_HALO, D_FF), lambda s, i: (s, 0))],
        out_shape=[jax.ShapeDtypeStruct((rows, D_MODEL), F32),
                   jax.ShapeDtypeStruct((nseq * FFN_HALO, D_FF), F32)],
        scratch_shapes=[pltpu.VMEM((FFN_HALO + tm, D_FF), F32)],
        compiler_params=_params("arbitrary", "arbitrary"),
        name="ffn_seq",
    )(x, ginit, gpre, win, dww, dwb, wout, gpost)


def _ffn_step_kernel(x_ref, st_ref, gpre_ref, win_ref, dww_ref, dwb_ref, wout_ref, gpost_ref,
                     y_ref, gnew_ref, gbuf_ref):
    nq, grp, _ = x_ref.shape
    nh = st_ref.shape[0]
    rows = nq * grp
    x = x_ref[...].reshape(rows, D_MODEL)
    h = _rms(x, gpre_ref[...]).astype(BF16)
    up = jnp.dot(h, win_ref[:, 0:D_FF], preferred_element_type=F32)
    gate = jnp.dot(h, win_ref[:, D_FF:2 * D_FF], preferred_element_type=F32)
    gbuf_ref[0:nh * grp, :] = st_ref[...].reshape(nh * grp, D_FF)
    gbuf_ref[nh * grp:nh * grp + rows, :] = gate
    g = (dww_ref[0:1, :] * gbuf_ref[pl.ds((nh - 2) * grp, rows), :]
         + dww_ref[1:2, :] * gbuf_ref[pl.ds((nh - 1) * grp, rows), :]
         + dww_ref[2:3, :] * gate + dwb_ref[...])
    y_ref[...] = _ffn_tail(x, up, g, wout_ref, gpost_ref).reshape(nq, grp, D_MODEL)
    ntail = gnew_ref.shape[0]
    gnew_ref[...] = gbuf_ref[pl.ds((nh + nq - ntail) * grp, ntail * grp), :].reshape(ntail, grp, D_FF)


def _ffn_step(x, state, gpre, win, dww, dwb, wout, gpost, group):
    nq, nseq, _ = x.shape
    nh = state.shape[0]
    blk = lambda t, width: pl.BlockSpec((t, group, width), lambda i: (0, i, 0))
    return pl.pallas_call(
        _ffn_step_kernel,
        grid=(nseq // group,),
        in_specs=[blk(nq, D_MODEL), blk(nh, D_FF), _const_spec((1, D_MODEL)),
                  _const_spec((D_MODEL, 2 * D_FF)), _const_spec((FFN_KERNEL, D_FF)),
                  _const_spec((1, D_FF)), _const_spec((D_FF, D_MODEL)), _const_spec((1, D_MODEL))],
        out_specs=[blk(nq, D_MODEL), blk(nh, D_FF)],
        out_shape=[jax.ShapeDtypeStruct((nq, nseq, D_MODEL), F32),
                   jax.ShapeDtypeStruct((nh, nseq, D_FF), F32)],
        scratch_shapes=[pltpu.VMEM(((nh + nq) * group, D_FF), F32)],
        compiler_params=_params("parallel"),
        name="ffn_step",
    )(x, state, gpre, win, dww, dwb, wout, gpost)


def _layer(xp, xs, pool_k, pool_v, st_conv, st_ffn, page_table, meta, norm_mix_pre, norm_mix_post,
           w_in, sb_bias, conv_dw_w, conv_dw_b, conv_ln_g, conv_ln_b, w_conv_out, w_attn_out,
           w_mix_out, norm_ffn_pre, norm_ffn_post, w_ffn_in, ffn_dw_w, ffn_dw_b, w_ffn_out):
    batch, seq, _ = xp.shape
    nseq, nq, _ = xs.shape
    vec = lambda a: a.reshape(1, -1).astype(F32)
    w_in_b, wa_b, wc_b, wm_b = (w.astype(BF16) for w in (w_in, w_attn_out, w_conv_out, w_mix_out))
    wfi_b, wfo_b = w_ffn_in.astype(BF16), w_ffn_out.astype(BF16)
    bias = sb_bias.reshape(1, HEADS).astype(F32)
    conv_w = (conv_dw_w, vec(conv_dw_b), vec(conv_ln_g), vec(conv_ln_b))
    ffn_w = (vec(norm_ffn_pre), wfi_b, ffn_dw_w, vec(ffn_dw_b), wfo_b, vec(norm_ffn_post))

    x_rows = xp.reshape(batch * seq, D_MODEL)
    head = jnp.concatenate([jnp.zeros((HEAD_PAD, D_MODEL), F32), meta.astype(F32)], axis=0)
    m_rows = jnp.concatenate([head, jnp.swapaxes(xs, 0, 1).reshape(nq * nseq, D_MODEL)], axis=0)
    tmajor = lambda a: a.reshape(nq, nseq, a.shape[-1])
    smajor = lambda a: jnp.swapaxes(tmajor(a), 0, 1)

    q_x, k_x, v_x, kb_x, vb_x, u_x, ga_x, gc_x = _inproj(x_rows, vec(norm_mix_pre), w_in_b, 256)
    q_m, k_m, v_m, kb_m, vb_m, u_m, ga_m, gc_m = _inproj(m_rows, vec(norm_mix_pre), w_in_b, 128)
    ht = HEAD_TILE

    att_x = _prompt_attn(bias, q_x, kb_x, vb_x, kb_m[:ht], vb_m[:ht], batch, seq)
    att_h = _head_attn(bias, q_m[:ht], kb_m[:ht], vb_m[:ht])
    k_s = smajor(k_m[ht:])
    v_s = smajor(v_m[ht:])
    att_s = _sample_attn(page_table, bias, smajor(q_m[ht:].astype(F32)), k_s, v_s, pool_k, pool_v)
    att_s = jnp.swapaxes(att_s, 0, 1).reshape(nq * nseq, SB_WIDTH).astype(BF16)
    att_m = jnp.concatenate([att_h, att_s], axis=0)

    u_h = u_m[:ht]
    u_s = tmajor(u_m[ht:])
    st_conv_t = jnp.swapaxes(st_conv, 0, 1)
    c_x = _conv_seq(u_x, u_h[ht - CONV_HALO:], *conv_w, nseq=batch, tc=256)
    c_h = _conv_seq(u_h, jnp.zeros((CONV_HALO, CONV_CH), F32), *conv_w, nseq=1, tc=ht)
    c_s = _conv_step(u_s, st_conv_t, *conv_w, group=32)
    c_m = jnp.concatenate([c_h, c_s.reshape(nq * nseq, CONV_CH)], axis=0)

    mix_w = (wa_b, wc_b, wm_b, vec(norm_mix_post))
    xm_x = _mix(x_rows, att_x, c_x, ga_x, gc_x, *mix_w, tm=256)
    xm_m = _mix(m_rows, att_m, c_m, ga_m, gc_m, *mix_w, tm=128)

    _, g_h = _ffn_seq(xm_m[HEAD_PAD:ht], jnp.zeros((FFN_HALO, D_FF), F32), *ffn_w, nseq=1, tm=N_META)
    y_x, g_x = _ffn_seq(xm_x, g_h, *ffn_w, nseq=batch, tm=256)
    y_s, gate_s = _ffn_step(tmajor(xm_m[ht:]), jnp.swapaxes(st_ffn, 0, 1), *ffn_w, group=32)

    k_meta = jnp.broadcast_to(k_m[HEAD_PAD:ht][None], (batch, N_META, SB_WIDTH))
    v_meta = jnp.broadcast_to(v_m[HEAD_PAD:ht][None], (batch, N_META, SB_WIDTH))
    k_p = jnp.concatenate([k_meta, k_x.reshape(batch, seq, SB_WIDTH)], axis=1)
    v_p = jnp.concatenate([v_meta, v_x.reshape(batch, seq, SB_WIDTH)], axis=1)
    total = N_META + seq
    conv_p = u_x.reshape(batch, seq, CONV_CH)[:, seq - (CONV_KERNEL - 1):]
    ffn_p = g_x.reshape(batch, FFN_HALO, D_FF)[:, FFN_HALO - (FFN_KERNEL - 1):]
    conv_s = jnp.swapaxes(jnp.concatenate([st_conv_t, u_s], axis=0)[nq:], 0, 1)
    ffn_s = jnp.swapaxes(gate_s, 0, 1)
    hd = lambda a, n, t: a.reshape(n, t, HEADS, HEAD_DIM)
    return (y_x.reshape(batch, seq, D_MODEL), jnp.swapaxes(y_s, 0, 1),
            hd(k_p, batch, total), hd(v_p, batch, total), conv_p, ffn_p,
            hd(k_s, nseq, nq), hd(v_s, nseq, nq), conv_s, ffn_s)


def kernel(x_prompt, x_sample, cache_k, cache_v, state_conv, state_ffn, page_table, meta_tokens, norm_mix_pre, norm_mix_post, w_in, sb_bias, conv_dw_w, conv_dw_b, conv_ln_g, conv_ln_b, w_conv_out, w_attn_out, w_mix_out, norm_ffn_pre, norm_ffn_post, w_ffn_in, ffn_dw_w, ffn_dw_b, w_ffn_out):
    depth = w_in.shape[0]
    assert depth == 1, "the prompt/sample layer pipeline below is written for a single layer"
    outs = _layer(x_prompt, x_sample,
                  jnp.transpose(cache_k[0], (0, 2, 3, 1)), jnp.transpose(cache_v[0], (0, 2, 3, 1)),
                  state_conv[0], state_ffn[0], page_table, meta_tokens,
                  norm_mix_pre[0], norm_mix_post[0], w_in[0], sb_bias[0], conv_dw_w[0], conv_dw_b[0],
                  conv_ln_g[0], conv_ln_b[0], w_conv_out[0], w_attn_out[0], w_mix_out[0],
                  norm_ffn_pre[0], norm_ffn_post[0], w_ffn_in[0], ffn_dw_w[0], ffn_dw_b[0], w_ffn_out[0])
    y_p, y_s, k_p, v_p, conv_p, ffn_p, k_s, v_s, conv_s, ffn_s = outs
    lead = lambda a: a[None]
    return (y_p, y_s, lead(k_p), lead(v_p), lead(conv_p), lead(ffn_p), lead(k_s), lead(v_s),
            lead(conv_s), lead(ffn_s))
```

```python
import functools

import jax
import jax.numpy as jnp
from jax import lax
from jax.experimental import pallas as pl
from jax.experimental.pallas import tpu as pltpu

F32 = jnp.float32
BF16 = jnp.bfloat16

D_MODEL = 1024
N_META = 16
HEADS = 8
HEAD_DIM = 64
SB_WIDTH = HEADS * HEAD_DIM
CONV_CH = 512
CONV_KERNEL = 31
FFN_KERNEL = 3
D_FF = 2816
PAGE = 128
RMS_EPS = 1e-6
LN_EPS = 1e-5

LANES = 128
SUBLANES = 8
HEAD_TILE = 128
HEAD_PAD = HEAD_TILE - N_META
CONV_HALO = 32
FFN_HALO = 8
VMEM_LIMIT = 56 * 1024 * 1024

ATT_TQ = 256
ATT_TK = 256
PAGES_PER_STEP = 16


def _params(*sem):
    return pltpu.CompilerParams(dimension_semantics=sem, vmem_limit_bytes=VMEM_LIMIT)


def _const_spec(shape):
    zeros = (0,) * len(shape)
    return pl.BlockSpec(shape, lambda *_: zeros)


def _rms(x, g):
    return x * lax.rsqrt(jnp.mean(x * x, axis=-1, keepdims=True) + RMS_EPS) * g


N_PROJ = 3 * SB_WIDTH + 2 * CONV_CH


def _inproj_kernel(x_ref, g_ref, w_ref, q_ref, k_ref, v_ref, kb_ref, vb_ref, u_ref):
    h = _rms(x_ref[...], g_ref[...]).astype(BF16)

    def proj(lo, hi):
        return jnp.dot(h, w_ref[:, lo:hi], preferred_element_type=F32)

    w = SB_WIDTH
    q_ref[...] = (proj(0, w) * (HEAD_DIM ** -0.5)).astype(BF16)
    k = proj(w, 2 * w)
    v = proj(2 * w, 3 * w)
    k_ref[...] = k
    v_ref[...] = v
    kb_ref[...] = k.astype(BF16)
    vb_ref[...] = v.astype(BF16)
    c0 = 3 * w
    u_ref[...] = proj(c0, c0 + CONV_CH) * jax.nn.sigmoid(proj(c0 + CONV_CH, c0 + 2 * CONV_CH))


def _inproj(x, g, w_bf, tm):
    rows = x.shape[0]
    row = lambda width: pl.BlockSpec((tm, width), lambda i: (i, 0))
    shp = lambda width, dt: jax.ShapeDtypeStruct((rows, width), dt)
    return pl.pallas_call(
        _inproj_kernel,
        grid=(rows // tm,),
        in_specs=[row(D_MODEL), _const_spec((1, D_MODEL)), _const_spec((D_MODEL, N_PROJ))],
        out_specs=[row(SB_WIDTH)] * 5 + [row(CONV_CH)],
        out_shape=[shp(SB_WIDTH, BF16), shp(SB_WIDTH, F32), shp(SB_WIDTH, F32), shp(SB_WIDTH, BF16),
                   shp(SB_WIDTH, BF16), shp(CONV_CH, F32)],
        compiler_params=_params("parallel"),
        name="inproj",
    )(x, g, w_bf)


LOG2E = 1.4426950408889634


def _suffix_matrix(tk):
    s = lax.broadcasted_iota(jnp.int32, (tk, tk), 0)
    j = lax.broadcasted_iota(jnp.int32, (tk, tk), 1)
    u = -((s > j).astype(BF16))
    return jnp.concatenate([u, u], axis=0)


def _sb_weights(zs, masks, u2n, carry):
    tq = zs[0].shape[0]
    nls_l, lb_l, parts = [], [], []
    for z, mask in zip(zs, masks):
        nls = jnp.maximum(z, 0.0) + jnp.log(1.0 + jnp.exp2(jnp.abs(z) * (-LOG2E)))
        lb_l.append(z - nls)
        if mask is not None:
            nls = jnp.where(mask, nls, 0.0)
        hi = nls.astype(BF16)
        lo = (nls - hi.astype(F32)).astype(BF16)
        parts.append(jnp.concatenate([hi, lo], axis=1))
        nls_l.append(nls)
    stacked = parts[0] if len(parts) == 1 else jnp.concatenate(parts, axis=0)
    later_all = jnp.dot(stacked, u2n, preferred_element_type=F32)
    ws = []
    for n, (nls, lb, mask) in enumerate(zip(nls_l, lb_l, masks)):
        later = later_all[n * tq:(n + 1) * tq] + carry
        w = jnp.exp2((lb + later) * LOG2E)
        if mask is not None:
            w = jnp.where(mask, w, 0.0)
        ws.append(w.astype(BF16))
        carry = later[:, 0:1] - nls[:, 0:1]
    return ws, carry


def _qk(q, k):
    return lax.dot_general(q, k, (((1,), (1,)), ((), ())), preferred_element_type=F32)


def _head_key_mask(tq):
    col = lax.broadcasted_iota(jnp.int32, (tq, HEAD_TILE), 1)
    return col >= HEAD_PAD


def _prompt_attn_kernel(bias_ref, q_ref, k_ref, v_ref, kh_ref, vh_ref, u2_ref, u2h_ref, o_ref,
                        acc_ref, carry_ref):
    hp = pl.program_id(1)
    i = pl.program_id(2)
    tq, tk = ATT_TQ, ATT_TK
    lane = lax.broadcasted_iota(jnp.int32, (1, LANES), 1)
    row = lax.broadcasted_iota(jnp.int32, (tq, tk), 0)
    col = lax.broadcasted_iota(jnp.int32, (tq, tk), 1)
    causal = col < row
    head_lanes = [(lane // HEAD_DIM) == s for s in range(2)]
    qm = [jnp.where(hl, q_ref[...], jnp.zeros_like(q_ref[...])) for hl in head_lanes]
    bias = [bias_ref[0, 2 * hp + s] for s in range(2)]

    def attend(k_slab, v_slab, masks, u2, first):
        n = len(masks)
        width = k_slab.shape[0] // n
        for s in range(2):
            z = _qk(qm[s], k_slab) + bias[s]
            zs = [z[:, b * width:(b + 1) * width] for b in reversed(range(n))]
            carry = jnp.zeros((tq, 1), F32) if first else carry_ref[s]
            ws, carry = _sb_weights(zs, masks, u2, carry)
            w = ws[0] if n == 1 else jnp.concatenate(ws[::-1], axis=1)
            pv = jnp.dot(w, v_slab, preferred_element_type=F32)
            acc_ref[s] = pv if first else acc_ref[s] + pv
            carry_ref[s] = carry

    def slab(ref, first_blk, nblk):
        return ref[pl.ds(pl.multiple_of(first_blk * tk, tk), nblk * tk), :]

    attend(slab(k_ref, i, 1), slab(v_ref, i, 1), [causal], u2_ref[...], True)
    odd = jnp.bitwise_and(i, 1)

    @pl.when(odd == 1)
    def _():
        attend(slab(k_ref, i - 1, 1), slab(v_ref, i - 1, 1), [None], u2_ref[...], False)

    def body(j, _):
        blk = i - odd - 2 - 2 * j
        attend(slab(k_ref, blk, 2), slab(v_ref, blk, 2), [None, None], u2_ref[...], False)
        return 0

    lax.fori_loop(0, jnp.right_shift(i, 1), body, 0)
    attend(kh_ref[...], vh_ref[...], [_head_key_mask(tq)], u2h_ref[...], False)
    o_ref[...] = jnp.where(head_lanes[0], acc_ref[0], acc_ref[1]).astype(o_ref.dtype)


def _prompt_attn(bias, q, kb, vb, kh, vh, batch, seq):
    nq = seq // ATT_TQ
    pair = lambda rows, imap: pl.BlockSpec((rows, LANES), imap)
    return pl.pallas_call(
        _prompt_attn_kernel,
        grid=(batch, HEADS // 2, nq),
        in_specs=[pl.BlockSpec(memory_space=pltpu.SMEM),
                  pair(ATT_TQ, lambda b, hp, i: (b * nq + i, hp)),
                  pair(seq, lambda b, hp, i: (b, hp)),
                  pair(seq, lambda b, hp, i: (b, hp)),
                  pair(HEAD_TILE, lambda b, hp, i: (0, hp)),
                  pair(HEAD_TILE, lambda b, hp, i: (0, hp)),
                  _const_spec((2 * ATT_TK, ATT_TK)),
                  _const_spec((2 * HEAD_TILE, HEAD_TILE))],
        out_specs=pair(ATT_TQ, lambda b, hp, i: (b * nq + i, hp)),
        out_shape=jax.ShapeDtypeStruct((batch * seq, SB_WIDTH), BF16),
        scratch_shapes=[pltpu.VMEM((2, ATT_TQ, LANES), F32), pltpu.VMEM((2, ATT_TQ, 1), F32)],
        compiler_params=_params("parallel", "parallel", "parallel"),
        name="prompt_attn",
    )(bias, q, kb, vb, kh, vh, _suffix_matrix(ATT_TK), _suffix_matrix(HEAD_TILE))


def _head_attn_kernel(bias_ref, q_ref, k_ref, v_ref, u2_ref, o_ref):
    hp = pl.program_id(0)
    t = HEAD_TILE
    lane = lax.broadcasted_iota(jnp.int32, (1, LANES), 1)
    row = lax.broadcasted_iota(jnp.int32, (t, t), 0)
    col = lax.broadcasted_iota(jnp.int32, (t, t), 1)
    mask = (col < row) & (col >= HEAD_PAD)
    outs = []
    for s in range(2):
        head_lanes = (lane // HEAD_DIM) == s
        qm = jnp.where(head_lanes, q_ref[...], jnp.zeros_like(q_ref[...]))
        z = _qk(qm, k_ref[...]) + bias_ref[0, 2 * hp + s]
        (w,), _ = _sb_weights([z], [mask], u2_ref[...], jnp.zeros((t, 1), F32))
        outs.append((head_lanes, jnp.dot(w, v_ref[...], preferred_element_type=F32)))
    o_ref[...] = jnp.where(outs[0][0], outs[0][1], outs[1][1]).astype(o_ref.dtype)


def _head_attn(bias, q, kb, vb):
    pair = pl.BlockSpec((HEAD_TILE, LANES), lambda hp: (0, hp))
    return pl.pallas_call(
        _head_attn_kernel,
        grid=(HEADS // 2,),
        in_specs=[pl.BlockSpec(memory_space=pltpu.SMEM), pair, pair, pair,
                  _const_spec((2 * HEAD_TILE, HEAD_TILE))],
        out_specs=pair,
        out_shape=jax.ShapeDtypeStruct((HEAD_TILE, SB_WIDTH), BF16),
        compiler_params=_params("parallel"),
        name="head_attn",
    )(bias, q, kb, vb, _suffix_matrix(HEAD_TILE))


def _sample_attn_kernel(pt_ref, bias_ref, q_ref, kn_ref, vn_ref, *rest):
    npg = PAGES_PER_STEP
    k_refs = rest[:npg]
    v_refs = rest[npg:2 * npg]
    u2_ref, o_ref, qb_ref, bias_rows_ref, acc_ref, carry_ref, new_ref = rest[2 * npg:]
    j = pl.program_id(1)
    nq = q_ref.shape[0]
    rows = HEADS * nq
    row_head = lax.broadcasted_iota(jnp.int32, (rows, SB_WIDTH), 0) // nq
    lane_head = lax.broadcasted_iota(jnp.int32, (rows, SB_WIDTH), 1) // HEAD_DIM

    def pages(refs):
        return jnp.concatenate([r[...].reshape(SB_WIDTH, PAGE).astype(BF16) for r in refs], axis=1)

    @pl.when(j == 0)
    def _():
        q = q_ref[...].astype(BF16)
        qrep = jnp.concatenate([q] * HEADS, axis=0)
        qb_ref[...] = jnp.where(row_head == lane_head, qrep, jnp.zeros_like(qrep))
        bias_rows_ref[...] = jnp.concatenate(
            [jnp.full((nq, 1), bias_ref[0, h], F32) for h in range(HEADS)], axis=0)
        new_ref[...] = jnp.zeros_like(new_ref)
        new_ref[0, 0:nq, :] = kn_ref[...]
        new_ref[1, 0:nq, :] = vn_ref[...]
        qi = lax.broadcasted_iota(jnp.int32, (rows, PAGE), 0) % nq
        ki = lax.broadcasted_iota(jnp.int32, (rows, PAGE), 1)
        z = _qk(qb_ref[...], new_ref[0].astype(BF16)) + bias_rows_ref[...]
        (w,), carry = _sb_weights([z], [ki < qi], u2_ref[...], jnp.zeros((rows, 1), F32))
        acc_ref[...] = jnp.dot(w, new_ref[1].astype(BF16), preferred_element_type=F32)
        carry_ref[...] = carry

    z = jnp.dot(qb_ref[...], pages(k_refs), preferred_element_type=F32) + bias_rows_ref[...]
    zs = [z[:, r * PAGE:(r + 1) * PAGE] for r in range(npg)]
    ws, carry = _sb_weights(zs, [None] * npg, u2_ref[...], carry_ref[...])
    acc_ref[...] += _qk(jnp.concatenate(ws, axis=1), pages(v_refs))
    carry_ref[...] = carry

    @pl.when(j == pl.num_programs(1) - 1)
    def _():
        out = jnp.zeros((nq, SB_WIDTH), F32)
        for h in range(HEADS):
            out = out + jnp.where(lane_head[0:nq] == h, acc_ref[h * nq:(h + 1) * nq, :], 0.0)
        o_ref[...] = out


def _sample_attn(page_table, bias, q, k_new, v_new, pool_k, pool_v):
    nseq, nq, _ = q.shape
    npages = page_table.shape[1]
    npg = PAGES_PER_STEP
    nsteps = npages // npg
    rows = HEADS * nq
    seq_spec = pl.BlockSpec((None, nq, SB_WIDTH), lambda s, j, pt: (s, 0, 0))

    def page_spec(r):
        def imap(s, j, pt):
            return (pt[s * npages + (npages - 1 - (j * npg + r))], 0, 0, 0)
        return pl.BlockSpec((None, HEADS, HEAD_DIM, PAGE), imap)

    grid_spec = pltpu.PrefetchScalarGridSpec(
        num_scalar_prefetch=1,
        grid=(nseq, nsteps),
        in_specs=[pl.BlockSpec(memory_space=pltpu.SMEM), seq_spec, seq_spec, seq_spec]
                 + [page_spec(r) for r in range(npg)] * 2
                 + [pl.BlockSpec((2 * PAGE, PAGE), lambda s, j, pt: (0, 0))],
        out_specs=seq_spec,
        scratch_shapes=[pltpu.VMEM((rows, SB_WIDTH), BF16), pltpu.VMEM((rows, 1), F32),
                        pltpu.VMEM((rows, SB_WIDTH), F32), pltpu.VMEM((rows, 1), F32),
                        pltpu.VMEM((2, PAGE, SB_WIDTH), F32)],
    )
    return pl.pallas_call(
        _sample_attn_kernel,
        grid_spec=grid_spec,
        out_shape=jax.ShapeDtypeStruct((nseq, nq, SB_WIDTH), F32),
        compiler_params=_params("parallel", "arbitrary"),
        name="sample_attn",
    )(page_table.reshape(-1), bias, q, k_new, v_new, *([pool_k] * npg), *([pool_v] * npg),
      _suffix_matrix(PAGE))


def _ln_swish(c, g, b):
    mu = jnp.mean(c, axis=-1, keepdims=True)
    d = c - mu
    var = jnp.mean(d * d, axis=-1, keepdims=True)
    y = d * lax.rsqrt(var + LN_EPS) * g + b
    return y * jax.nn.sigmoid(y)


CONV_ROWS = 32


def _conv_seq_kernel(u_ref, prev_ref, first_ref, w_ref, b_ref, g_ref, beta_ref, o_ref, hist_ref,
                     shift_ref):
    tc = u_ref.shape[0]
    first = pl.program_id(1) == 0
    hist_ref[0:CONV_HALO, :] = jnp.where(first, first_ref[...], prev_ref[...])
    hist_ref[CONV_HALO:CONV_HALO + tc, :] = u_ref[...]
    nshift = shift_ref.shape[1]
    for r in range(1, SUBLANES):
        shift_ref[r - 1] = hist_ref[pl.ds(r, nshift), :]
    base = CONV_HALO - (CONV_KERNEL - 1)
    for r0 in range(0, tc, CONV_ROWS):
        acc = jnp.broadcast_to(b_ref[...], (CONV_ROWS, CONV_CH))
        for k in range(CONV_KERNEL):
            r = (base + k) % SUBLANES
            start = r0 + base + k - r
            if r == 0:
                window = hist_ref[pl.ds(start, CONV_ROWS), :]
            else:
                window = shift_ref[r - 1, pl.ds(start, CONV_ROWS), :]
            acc = acc + w_ref[k:k + 1, :] * window
        o_ref[r0:r0 + CONV_ROWS, :] = _ln_swish(acc, g_ref[...], beta_ref[...]).astype(o_ref.dtype)


def _conv_seq(u, first_hist, w, b, g, beta, nseq, tc):
    rows = u.shape[0]
    nt = rows // (nseq * tc)
    per = tc // CONV_HALO
    vec = _const_spec((1, CONV_CH))
    return pl.pallas_call(
        _conv_seq_kernel,
        grid=(nseq, nt),
        in_specs=[pl.BlockSpec((tc, CONV_CH), lambda s, i: (s * nt + i, 0)),
                  pl.BlockSpec((CONV_HALO, CONV_CH),
                               lambda s, i: (jnp.maximum((s * nt + i) * per - 1, 0), 0)),
                  _const_spec((CONV_HALO, CONV_CH)), _const_spec((CONV_KERNEL, CONV_CH)), vec, vec, vec],
        out_specs=pl.BlockSpec((tc, CONV_CH), lambda s, i: (s * nt + i, 0)),
        out_shape=jax.ShapeDtypeStruct((rows, CONV_CH), BF16),
        scratch_shapes=[pltpu.VMEM((CONV_HALO + tc, CONV_CH), F32),
                        pltpu.VMEM((SUBLANES - 1, CONV_HALO + tc - SUBLANES, CONV_CH), F32)],
        compiler_params=_params("parallel", "parallel"),
        name="conv_seq",
    )(u, u, first_hist, w, b, g, beta)


def _conv_step_kernel(u_ref, st_ref, w_ref, b_ref, g_ref, beta_ref, o_ref, hist_ref):
    nq, grp, _ = u_ref.shape
    nh = st_ref.shape[0]
    hist_ref[0:nh] = st_ref[...]
    hist_ref[nh:nh + nq] = u_ref[...]
    for t in range(nq):
        acc = jnp.broadcast_to(b_ref[...], (grp, CONV_CH))
        for k in range(CONV_KERNEL):
            acc = acc + w_ref[k:k + 1, :] * hist_ref[t + k]
        o_ref[t] = _ln_swish(acc, g_ref[...], beta_ref[...]).astype(o_ref.dtype)


def _conv_step(u, state, w, b, g, beta, group):
    nq, nseq, _ = u.shape
    nh = state.shape[0]
    vec = _const_spec((1, CONV_CH))
    blk = lambda t: pl.BlockSpec((t, group, CONV_CH), lambda i: (0, i, 0))
    return pl.pallas_call(
        _conv_step_kernel,
        grid=(nseq // group,),
        in_specs=[blk(nq), blk(nh), _const_spec((CONV_KERNEL, CONV_CH)), vec, vec, vec],
        out_specs=blk(nq),
        out_shape=jax.ShapeDtypeStruct((nq, nseq, CONV_CH), BF16),
        scratch_shapes=[pltpu.VMEM((nh + nq, group, CONV_CH), F32)],
        compiler_params=_params("parallel"),
        name="conv_step",
    )(u, state, w, b, g, beta)


def _mix_kernel(x_ref, att_ref, c_ref, gpre_ref, wg_ref, wa_ref, wc_ref, wm_ref, g_ref, o_ref):
    x = x_ref[...]
    h = _rms(x, gpre_ref[...]).astype(BF16)
    gate_att = jax.nn.sigmoid(jnp.dot(h, wg_ref[:, 0:D_MODEL], preferred_element_type=F32))
    a = jnp.dot(att_ref[...], wa_ref[...], preferred_element_type=F32)
    merged = gate_att * a
    gate_conv = jax.nn.sigmoid(jnp.dot(h, wg_ref[:, D_MODEL:2 * D_MODEL], preferred_element_type=F32))
    c = jnp.dot(c_ref[...], wc_ref[...], preferred_element_type=F32)
    merged = (merged + gate_conv * c).astype(BF16)
    m = jnp.dot(merged, wm_ref[...], preferred_element_type=F32)
    o_ref[...] = x + _rms(m, g_ref[...])


def _mix(x, att, cact, gpre, wg, wa, wc, wm, g, tm):
    rows = x.shape[0]
    row = lambda width: pl.BlockSpec((tm, width), lambda i: (i, 0))
    return pl.pallas_call(
        _mix_kernel,
        grid=(rows // tm,),
        in_specs=[row(D_MODEL), row(SB_WIDTH), row(CONV_CH), _const_spec((1, D_MODEL)),
                  _const_spec((D_MODEL, 2 * D_MODEL)),
                  _const_spec((SB_WIDTH, D_MODEL)), _const_spec((CONV_CH, D_MODEL)),
                  _const_spec((D_MODEL, D_MODEL)), _const_spec((1, D_MODEL))],
        out_specs=row(D_MODEL),
        out_shape=jax.ShapeDtypeStruct((rows, D_MODEL), F32),
        compiler_params=_params("parallel"),
        name="mix",
    )(x, att, cact, gpre, wg, wa, wc, wm, g)


def _ffn_tail(x, up, g, wout_ref, gpost_ref):
    act = (jax.nn.gelu(g, approximate=True) * up).astype(BF16)
    o = jnp.dot(act, wout_ref[...], preferred_element_type=F32)
    return x + _rms(o, gpost_ref[...])


def _ffn_seq_kernel(x_ref, ginit_ref, gpre_ref, win_ref, dww_ref, dwb_ref, wout_ref, gpost_ref,
                    y_ref, gtail_ref, gbuf_ref):
    tm = x_ref.shape[0]

    @pl.when(pl.program_id(1) == 0)
    def _():
        gbuf_ref[0:FFN_HALO, :] = ginit_ref[...]

    x = x_ref[...]
    h = _rms(x, gpre_ref[...]).astype(BF16)
    up = jnp.dot(h, win_ref[:, 0:D_FF], preferred_element_type=F32)
    gate = jnp.dot(h, win_ref[:, D_FF:2 * D_FF], preferred_element_type=F32)
    gbuf_ref[FFN_HALO:FFN_HALO + tm, :] = gate
    g = (dww_ref[0:1, :] * gbuf_ref[pl.ds(FFN_HALO - 2, tm), :]
         + dww_ref[1:2, :] * gbuf_ref[pl.ds(FFN_HALO - 1, tm), :]
         + dww_ref[2:3, :] * gate + dwb_ref[...])
    y_ref[...] = _ffn_tail(x, up, g, wout_ref, gpost_ref)
    tail = gbuf_ref[pl.ds(tm, FFN_HALO), :]
    gbuf_ref[0:FFN_HALO, :] = tail
    gtail_ref[...] = tail


def _ffn_seq(x, ginit, gpre, win, dww, dwb, wout, gpost, nseq, tm):
    rows = x.shape[0]
    nt = rows // (nseq * tm)
    return pl.pallas_call(
        _ffn_seq_kernel,
        grid=(nseq, nt),
        in_specs=[pl.BlockSpec((tm, D_MODEL), lambda s, i: (s * nt + i, 0)),
                  _const_spec((FFN_HALO, D_FF)), _const_spec((1, D_MODEL)),
                  _const_spec((D_MODEL, 2 * D_FF)), _const_spec((FFN_KERNEL, D_FF)),
                  _const_spec((1, D_FF)), _const_spec((D_FF, D_MODEL)), _const_spec((1, D_MODEL))],
        out_specs=[pl.BlockSpec((tm, D_MODEL), lambda s, i: (s * nt + i, 0)),
                   pl.BlockSpec((FFN_HALO, D_FF), lambda s, i: (s, 0))],
        out_shape=[jax.ShapeDtypeStruct((rows, D_MODEL), F32),
                   jax.ShapeDtypeStruct((nseq * FFN_HALO, D_FF), F32)],
        scratch_shapes=[pltpu.VMEM((FFN_HALO + tm, D_FF), F32)],
        compiler_params=_params("arbitrary", "arbitrary"),
        name="ffn_seq",
    )(x, ginit, gpre, win, dww, dwb, wout, gpost)


def _ffn_step_kernel(x_ref, st_ref, gpre_ref, win_ref, dww_ref, dwb_ref, wout_ref, gpost_ref,
                     y_ref, gnew_ref, gbuf_ref):
    nq, grp, _ = x_ref.shape
    nh = st_ref.shape[0]
    rows = nq * grp
    x = x_ref[...].reshape(rows, D_MODEL)
    h = _rms(x, gpre_ref[...]).astype(BF16)
    up = jnp.dot(h, win_ref[:, 0:D_FF], preferred_element_type=F32)
    gate = jnp.dot(h, win_ref[:, D_FF:2 * D_FF], preferred_element_type=F32)
    gbuf_ref[0:nh * grp, :] = st_ref[...].reshape(nh * grp, D_FF)
    gbuf_ref[nh * grp:nh * grp + rows, :] = gate
    g = (dww_ref[0:1, :] * gbuf_ref[pl.ds((nh - 2) * grp, rows), :]
         + dww_ref[1:2, :] * gbuf_ref[pl.ds((nh - 1) * grp, rows), :]
         + dww_ref[2:3, :] * gate + dwb_ref[...])
    y_ref[...] = _ffn_tail(x, up, g, wout_ref, gpost_ref).reshape(nq, grp, D_MODEL)
    ntail = gnew_ref.shape[0]
    gnew_ref[...] = gbuf_ref[pl.ds((nh + nq - ntail) * grp, ntail * grp), :].reshape(ntail, grp, D_FF)


def _ffn_step(x, state, gpre, win, dww, dwb, wout, gpost, group):
    nq, nseq, _ = x.shape
    nh = state.shape[0]
    blk = lambda t, width: pl.BlockSpec((t, group, width), lambda i: (0, i, 0))
    return pl.pallas_call(
        _ffn_step_kernel,
        grid=(nseq // group,),
        in_specs=[blk(nq, D_MODEL), blk(nh, D_FF), _const_spec((1, D_MODEL)),
                  _const_spec((D_MODEL, 2 * D_FF)), _const_spec((FFN_KERNEL, D_FF)),
                  _const_spec((1, D_FF)), _const_spec((D_FF, D_MODEL)), _const_spec((1, D_MODEL))],
        out_specs=[blk(nq, D_MODEL), blk(nh, D_FF)],
        out_shape=[jax.ShapeDtypeStruct((nq, nseq, D_MODEL), F32),
                   jax.ShapeDtypeStruct((nh, nseq, D_FF), F32)],
        scratch_shapes=[pltpu.VMEM(((nh + nq) * group, D_FF), F32)],
        compiler_params=_params("parallel"),
        name="ffn_step",
    )(x, state, gpre, win, dww, dwb, wout, gpost)


def _layer(xp, xs, pool_k, pool_v, st_conv, st_ffn, page_table, meta, norm_mix_pre, norm_mix_post,
           w_in, sb_bias, conv_dw_w, conv_dw_b, conv_ln_g, conv_ln_b, w_conv_out, w_attn_out,
           w_mix_out, norm_ffn_pre, norm_ffn_post, w_ffn_in, ffn_dw_w, ffn_dw_b, w_ffn_out):
    batch, seq, _ = xp.shape
    nseq, nq, _ = xs.shape
    vec = lambda a: a.reshape(1, -1).astype(F32)
    w_in_b, wa_b, wc_b, wm_b = (w.astype(BF16) for w in (w_in, w_attn_out, w_conv_out, w_mix_out))
    wfi_b, wfo_b = w_ffn_in.astype(BF16), w_ffn_out.astype(BF16)
    bias = sb_bias.reshape(1, HEADS).astype(F32)
    conv_w = (conv_dw_w, vec(conv_dw_b), vec(conv_ln_g), vec(conv_ln_b))
    ffn_w = (vec(norm_ffn_pre), wfi_b, ffn_dw_w, vec(ffn_dw_b), wfo_b, vec(norm_ffn_post))

    x_rows = xp.reshape(batch * seq, D_MODEL)
    head = jnp.concatenate([jnp.zeros((HEAD_PAD, D_MODEL), F32), meta.astype(F32)], axis=0)
    m_rows = jnp.concatenate([head, jnp.swapaxes(xs, 0, 1).reshape(nq * nseq, D_MODEL)], axis=0)
    tmajor = lambda a: a.reshape(nq, nseq, a.shape[-1])
    smajor = lambda a: jnp.swapaxes(tmajor(a), 0, 1)

    w_proj, w_gate = w_in_b[:, :N_PROJ], w_in_b[:, N_PROJ:]
    q_x, k_x, v_x, kb_x, vb_x, u_x = _inproj(x_rows, vec(norm_mix_pre), w_proj, 512)
    q_m, k_m, v_m, kb_m, vb_m, u_m = _inproj(m_rows, vec(norm_mix_pre), w_proj, 128)
    ht = HEAD_TILE

    att_x = _prompt_attn(bias, q_x, kb_x, vb_x, kb_m[:ht], vb_m[:ht], batch, seq)
    att_h = _head_attn(bias, q_m[:ht], kb_m[:ht], vb_m[:ht])
    k_s = smajor(k_m[ht:])
    v_s = smajor(v_m[ht:])
    att_s = _sample_attn(page_table, bias, smajor(q_m[ht:].astype(F32)), k_s, v_s, pool_k, pool_v)
    att_s = jnp.swapaxes(att_s, 0, 1).reshape(nq * nseq, SB_WIDTH).astype(BF16)
    att_m = jnp.concatenate([att_h, att_s], axis=0)

    u_h = u_m[:ht]
    u_s = tmajor(u_m[ht:])
    st_conv_t = jnp.swapaxes(st_conv, 0, 1)
    c_x = _conv_seq(u_x, u_h[ht - CONV_HALO:], *conv_w, nseq=batch, tc=256)
    c_h = _conv_seq(u_h, jnp.zeros((CONV_HALO, CONV_CH), F32), *conv_w, nseq=1, tc=ht)
    c_s = _conv_step(u_s, st_conv_t, *conv_w, group=32)
    c_m = jnp.concatenate([c_h, c_s.reshape(nq * nseq, CONV_CH)], axis=0)

    mix_w = (vec(norm_mix_pre), w_gate, wa_b, wc_b, wm_b, vec(norm_mix_post))
    xm_x = _mix(x_rows, att_x, c_x, *mix_w, tm=512)
    xm_m = _mix(m_rows, att_m, c_m, *mix_w, tm=128)

    _, g_h = _ffn_seq(xm_m[HEAD_PAD:ht], jnp.zeros((FFN_HALO, D_FF), F32), *ffn_w, nseq=1, tm=N_META)
    y_x, g_x = _ffn_seq(xm_x, g_h, *ffn_w, nseq=batch, tm=512)
    y_s, gate_s = _ffn_step(tmajor(xm_m[ht:]), jnp.swapaxes(st_ffn, 0, 1), *ffn_w, group=32)

    k_meta = jnp.broadcast_to(k_m[HEAD_PAD:ht][None], (batch, N_META, SB_WIDTH))
    v_meta = jnp.broadcast_to(v_m[HEAD_PAD:ht][None], (batch, N_META, SB_WIDTH))
    k_p = jnp.concatenate([k_meta, k_x.reshape(batch, seq, SB_WIDTH)], axis=1)
    v_p = jnp.concatenate([v_meta, v_x.reshape(batch, seq, SB_WIDTH)], axis=1)
    total = N_META + seq
    conv_p = u_x.reshape(batch, seq, CONV_CH)[:, seq - (CONV_KERNEL - 1):]
    ffn_p = g_x.reshape(batch, FFN_HALO, D_FF)[:, FFN_HALO - (FFN_KERNEL - 1):]
    conv_s = jnp.swapaxes(jnp.concatenate([st_conv_t, u_s], axis=0)[nq:], 0, 1)
    ffn_s = jnp.swapaxes(gate_s, 0, 1)
    hd = lambda a, n, t: a.reshape(n, t, HEADS, HEAD_DIM)
    return (y_x.reshape(batch, seq, D_MODEL), jnp.swapaxes(y_s, 0, 1),
            hd(k_p, batch, total), hd(v_p, batch, total), conv_p, ffn_p,
            hd(k_s, nseq, nq), hd(v_s, nseq, nq), conv_s, ffn_s)


def kernel(x_prompt, x_sample, cache_k, cache_v, state_conv, state_ffn, page_table, meta_tokens, norm_mix_pre, norm_mix_post, w_in, sb_bias, conv_dw_w, conv_dw_b, conv_ln_g, conv_ln_b, w_conv_out, w_attn_out, w_mix_out, norm_ffn_pre, norm_ffn_post, w_ffn_in, ffn_dw_w, ffn_dw_b, w_ffn_out):
    depth = w_in.shape[0]
    assert depth == 1, "the prompt/sample layer pipeline below is written for a single layer"
    outs = _layer(x_prompt, x_sample,
                  jnp.transpose(cache_k[0], (0, 2, 3, 1)), jnp.transpose(cache_v[0], (0, 2, 3, 1)),
                  state_conv[0], state_ffn[0], page_table, meta_tokens,
                  norm_mix_pre[0], norm_mix_post[0], w_in[0], sb_bias[0], conv_dw_w[0], conv_dw_b[0],
                  conv_ln_g[0], conv_ln_b[0], w_conv_out[0], w_attn_out[0], w_mix_out[0],
                  norm_ffn_pre[0], norm_ffn_post[0], w_ffn_in[0], ffn_dw_w[0], ffn_dw_b[0], w_ffn_out[0])
    y_p, y_s, k_p, v_p, conv_p, ffn_p, k_s, v_s, conv_s, ffn_s = outs
    lead = lambda a: a[None]
    return (y_p, y_s, lead(k_p), lead(v_p), lead(conv_p), lead(ffn_p), lead(k_s), lead(v_s),
            lead(conv_s), lead(ffn_s))
```

```python
import functools

import jax
import jax.numpy as jnp
from jax import lax
from jax.experimental import pallas as pl
from jax.experimental.pallas import tpu as pltpu

F32 = jnp.float32
BF16 = jnp.bfloat16

D_MODEL = 1024
N_META = 16
HEADS = 8
HEAD_DIM = 64
SB_WIDTH = HEADS * HEAD_DIM
CONV_CH = 512
CONV_KERNEL = 31
FFN_KERNEL = 3
D_FF = 2816
PAGE = 128
RMS_EPS = 1e-6
LN_EPS = 1e-5

LANES = 128
SUBLANES = 8
HEAD_TILE = 128
HEAD_PAD = HEAD_TILE - N_META
CONV_HALO = 32
FFN_HALO = 8
VMEM_LIMIT = 56 * 1024 * 1024

ATT_TQ = 512
ATT_TK = 256
PAGES_PER_STEP = 16
PAGE_GROUP = 16


def _params(*sem):
    return pltpu.CompilerParams(dimension_semantics=sem, vmem_limit_bytes=VMEM_LIMIT)


def _const_spec(shape):
    zeros = (0,) * len(shape)
    return pl.BlockSpec(shape, lambda *_: zeros)


def _rms(x, g):
    return x * lax.rsqrt(jnp.mean(x * x, axis=-1, keepdims=True) + RMS_EPS) * g


N_PROJ = 3 * SB_WIDTH + 2 * CONV_CH


def _inproj_kernel(x_ref, g_ref, w_ref, q_ref, k_ref, v_ref, kb_ref, vb_ref, u_ref):
    h = _rms(x_ref[...], g_ref[...]).astype(BF16)

    def proj(lo, hi):
        return jnp.dot(h, w_ref[:, lo:hi], preferred_element_type=F32)

    w = SB_WIDTH
    q_ref[...] = (proj(0, w) * (HEAD_DIM ** -0.5)).astype(BF16)
    k = proj(w, 2 * w)
    v = proj(2 * w, 3 * w)
    k_ref[...] = k
    v_ref[...] = v
    kb_ref[...] = k.astype(BF16)
    vb_ref[...] = v.astype(BF16)
    c0 = 3 * w
    u_ref[...] = proj(c0, c0 + CONV_CH) * jax.nn.sigmoid(proj(c0 + CONV_CH, c0 + 2 * CONV_CH))


def _inproj(x, g, w_bf, tm):
    rows = x.shape[0]
    row = lambda width: pl.BlockSpec((tm, width), lambda i: (i, 0))
    shp = lambda width, dt: jax.ShapeDtypeStruct((rows, width), dt)
    return pl.pallas_call(
        _inproj_kernel,
        grid=(rows // tm,),
        in_specs=[row(D_MODEL), _const_spec((1, D_MODEL)), _const_spec((D_MODEL, N_PROJ))],
        out_specs=[row(SB_WIDTH)] * 5 + [row(CONV_CH)],
        out_shape=[shp(SB_WIDTH, BF16), shp(SB_WIDTH, F32), shp(SB_WIDTH, F32), shp(SB_WIDTH, BF16),
                   shp(SB_WIDTH, BF16), shp(CONV_CH, F32)],
        compiler_params=_params("parallel"),
        name="inproj",
    )(x, g, w_bf)


LOG2E = 1.4426950408889634
ROW_CHUNK = 64


def _suffix_matrix(tk):
    s = lax.broadcasted_iota(jnp.int32, (tk, tk), 0)
    j = lax.broadcasted_iota(jnp.int32, (tk, tk), 1)
    u = -((s > j).astype(BF16))
    return jnp.concatenate([u, u], axis=0)


def _sb_front(zs, masks, u2n):
    nls_l, lb_l, parts = [], [], []
    for z_all, mask_all in zip(zs, masks):
        tq = z_all.shape[0]
        for r0 in range(0, tq, min(tq, ROW_CHUNK)):
            r1 = r0 + min(tq, ROW_CHUNK)
            z = z_all[r0:r1]
            nls = jnp.maximum(z, 0.0) + jnp.log(1.0 + jnp.exp2(jnp.abs(z) * (-LOG2E)))
            lb_l.append(z - nls)
            if mask_all is not None:
                nls = jnp.where(_mask_rows(mask_all, r0, r1), nls, 0.0)
            hi = nls.astype(BF16)
            lo = (nls - hi.astype(F32)).astype(BF16)
            parts.append(jnp.concatenate([hi, lo], axis=1))
            nls_l.append(nls[:, 0:LANES])
    stacked = parts[0] if len(parts) == 1 else jnp.concatenate(parts, axis=0)
    cat = lambda xs, n: [jnp.concatenate(xs[i * n:(i + 1) * n], axis=0) if n > 1 else xs[i]
                         for i in range(len(zs))]
    per = len(parts) // len(zs)
    return cat(nls_l, per), cat(lb_l, per), jnp.dot(stacked, u2n, preferred_element_type=F32)


def _sb_back(front, masks, carry):
    nls0_l, lb_l, later_all = front
    tq = lb_l[0].shape[0]
    ws = []
    ch = min(tq, ROW_CHUNK)
    for n, (nls0, lb, mask) in enumerate(zip(nls0_l, lb_l, masks)):
        w_rows, carry_rows = [], []
        for r0 in range(0, tq, ch):
            later = later_all[n * tq + r0:n * tq + r0 + ch]
            if carry is not None:
                later = later + carry[r0:r0 + ch, 0:1]
            w = jnp.exp2((lb[r0:r0 + ch] + later) * LOG2E)
            if mask is not None:
                w = jnp.where(_mask_rows(mask, r0, r0 + ch), w, 0.0)
            w_rows.append(w.astype(BF16))
            carry_rows.append(later[:, 0:LANES] - nls0[r0:r0 + ch])
        ws.append(w_rows[0] if len(w_rows) == 1 else jnp.concatenate(w_rows, axis=0))
        carry = carry_rows[0] if len(carry_rows) == 1 else jnp.concatenate(carry_rows, axis=0)
    return ws, carry


def _sb_weights(zs, masks, u2n, carry):
    return _sb_back(_sb_front(zs, masks, u2n), masks, carry)


def _emit_skewed(nchains, stages):
    vals = [None] * nchains
    for t in range(nchains + len(stages) - 1):
        for k in reversed(range(len(stages))):
            c = t - k
            if 0 <= c < nchains:
                vals[c] = stages[k](c, vals[c])


def _qk(q, k):
    return lax.dot_general(q, k, (((1,), (1,)), ((), ())), preferred_element_type=F32)


def _head_key_mask():
    col = lax.broadcasted_iota(jnp.int32, (1, HEAD_TILE), 1)
    return col >= HEAD_PAD


def _mask_rows(mask, r0, r1):
    return mask if mask.shape[0] == 1 else mask[r0:r1]


def _prompt_attn_kernel(bias_ref, q_ref, k_ref, v_ref, kh_ref, vh_ref, u2_ref, u2h_ref, o_ref,
                        acc0_ref, acc1_ref, carry0_ref, carry1_ref):
    acc_refs = (acc0_ref, acc1_ref)
    carry_refs = (carry0_ref, carry1_ref)
    hp = pl.program_id(1)
    i = pl.program_id(2)
    tq, tk = ATT_TQ, ATT_TK
    lane = lax.broadcasted_iota(jnp.int32, (1, LANES), 1)
    row = lax.broadcasted_iota(jnp.int32, (tq, tk), 0)
    col = lax.broadcasted_iota(jnp.int32, (tq, tk), 1)
    head_lanes = [(lane // HEAD_DIM) == s for s in range(2)]
    qm = [jnp.where(hl, q_ref[...], jnp.zeros_like(q_ref[...])) for hl in head_lanes]
    bias = [bias_ref[0, 2 * hp + s] for s in range(2)]

    def attend(k_slab, v_slab, masks, u2, first):
        n = len(masks)
        width = k_slab.shape[0] // n
        rows = lambda slab, c: slab[(n - 1 - c // 2) * width:(n - c // 2) * width]
        carry = [None if first else r[...] for r in carry_refs]
        acc = [None if first else r[...] for r in acc_refs]

        def qk(c, _):
            return _qk(qm[c % 2], rows(k_slab, c)) + bias[c % 2]

        def front(c, z):
            return _sb_front([z], [masks[c // 2]], u2)

        def back(c, fr):
            s = c % 2
            (w,), carry[s] = _sb_back(fr, [masks[c // 2]], carry[s])
            pv = jnp.dot(w, rows(v_slab, c), preferred_element_type=F32)
            acc[s] = pv if acc[s] is None else acc[s] + pv

        _emit_skewed(2 * n, [qk, front, back])
        for s in range(2):
            acc_refs[s][...] = acc[s]
            carry_refs[s][...] = carry[s]

    def slab(ref, first_blk, nblk):
        return ref[pl.ds(pl.multiple_of(first_blk * tk, tk), nblk * tk), :]

    nkb = tq // tk
    assert nkb == 2
    diag_masks = [col + d * tk < row for d in reversed(range(nkb))]
    attend(slab(k_ref, nkb * i, nkb), slab(v_ref, nkb * i, nkb), diag_masks, u2_ref[...], True)

    def body(j, _):
        blk = nkb * (i - 1 - j)
        attend(slab(k_ref, blk, nkb), slab(v_ref, blk, nkb), [None] * nkb, u2_ref[...], False)
        return 0

    lax.fori_loop(0, i, body, 0)
    attend(kh_ref[...], vh_ref[...], [_head_key_mask()], u2h_ref[...], False)
    o_ref[...] = jnp.where(head_lanes[0], acc0_ref[...], acc1_ref[...]).astype(o_ref.dtype)


def _prompt_attn(bias, q, kb, vb, kh, vh, batch, seq):
    nq = seq // ATT_TQ
    pair = lambda rows, imap: pl.BlockSpec((rows, LANES), imap)
    return pl.pallas_call(
        _prompt_attn_kernel,
        grid=(batch, HEADS // 2, nq),
        in_specs=[pl.BlockSpec(memory_space=pltpu.SMEM),
                  pair(ATT_TQ, lambda b, hp, i: (b * nq + i, hp)),
                  pair(seq, lambda b, hp, i: (b, hp)),
                  pair(seq, lambda b, hp, i: (b, hp)),
                  pair(HEAD_TILE, lambda b, hp, i: (0, hp)),
                  pair(HEAD_TILE, lambda b, hp, i: (0, hp)),
                  _const_spec((2 * ATT_TK, ATT_TK)),
                  _const_spec((2 * HEAD_TILE, HEAD_TILE))],
        out_specs=pair(ATT_TQ, lambda b, hp, i: (b * nq + i, hp)),
        out_shape=jax.ShapeDtypeStruct((batch * seq, SB_WIDTH), BF16),
        scratch_shapes=[pltpu.VMEM((ATT_TQ, LANES), F32)] * 4,
        compiler_params=_params("parallel", "parallel", "parallel"),
        name="prompt_attn",
    )(bias, q, kb, vb, kh, vh, _suffix_matrix(ATT_TK), _suffix_matrix(HEAD_TILE))


def _head_attn_kernel(bias_ref, q_ref, k_ref, v_ref, u2_ref, o_ref):
    hp = pl.program_id(0)
    t = HEAD_TILE
    lane = lax.broadcasted_iota(jnp.int32, (1, LANES), 1)
    row = lax.broadcasted_iota(jnp.int32, (t, t), 0)
    col = lax.broadcasted_iota(jnp.int32, (t, t), 1)
    mask = (col < row) & (col >= HEAD_PAD)
    outs = []
    for s in range(2):
        head_lanes = (lane // HEAD_DIM) == s
        qm = jnp.where(head_lanes, q_ref[...], jnp.zeros_like(q_ref[...]))
        z = _qk(qm, k_ref[...]) + bias_ref[0, 2 * hp + s]
        (w,), _ = _sb_weights([z], [mask], u2_ref[...], None)
        outs.append((head_lanes, jnp.dot(w, v_ref[...], preferred_element_type=F32)))
    o_ref[...] = jnp.where(outs[0][0], outs[0][1], outs[1][1]).astype(o_ref.dtype)


def _head_attn(bias, q, kb, vb):
    pair = pl.BlockSpec((HEAD_TILE, LANES), lambda hp: (0, hp))
    return pl.pallas_call(
        _head_attn_kernel,
        grid=(HEADS // 2,),
        in_specs=[pl.BlockSpec(memory_space=pltpu.SMEM), pair, pair, pair,
                  _const_spec((2 * HEAD_TILE, HEAD_TILE))],
        out_specs=pair,
        out_shape=jax.ShapeDtypeStruct((HEAD_TILE, SB_WIDTH), BF16),
        compiler_params=_params("parallel"),
        name="head_attn",
    )(bias, q, kb, vb, _suffix_matrix(HEAD_TILE))


def _sample_attn_kernel(pt_ref, bias_ref, q_ref, kn_ref, vn_ref, *rest):
    npg = PAGES_PER_STEP
    k_refs = rest[:npg]
    v_refs = rest[npg:2 * npg]
    u2_ref, o_ref, qb_ref, bias_rows_ref, acc_ref, carry_ref, new_ref = rest[2 * npg:]
    j = pl.program_id(1)
    nq = q_ref.shape[0]
    rows = HEADS * nq
    row_head = lax.broadcasted_iota(jnp.int32, (rows, SB_WIDTH), 0) // nq
    lane_head = lax.broadcasted_iota(jnp.int32, (rows, SB_WIDTH), 1) // HEAD_DIM

    def pages(refs):
        return jnp.concatenate([r[...].reshape(SB_WIDTH, PAGE).astype(BF16) for r in refs], axis=1)

    @pl.when(j == 0)
    def _():
        q = q_ref[...].astype(BF16)
        qrep = jnp.concatenate([q] * HEADS, axis=0)
        qb_ref[...] = jnp.where(row_head == lane_head, qrep, jnp.zeros_like(qrep))
        bias_rows_ref[...] = jnp.concatenate(
            [jnp.full((nq, 1), bias_ref[0, h], F32) for h in range(HEADS)], axis=0)
        new_ref[...] = jnp.zeros_like(new_ref)
        new_ref[0, 0:nq, :] = kn_ref[...]
        new_ref[1, 0:nq, :] = vn_ref[...]
        qi = lax.broadcasted_iota(jnp.int32, (rows, PAGE), 0) % nq
        ki = lax.broadcasted_iota(jnp.int32, (rows, PAGE), 1)
        z = _qk(qb_ref[...], new_ref[0].astype(BF16)) + bias_rows_ref[...]
        (w,), carry = _sb_weights([z], [ki < qi], u2_ref[...], None)
        acc_ref[...] = jnp.dot(w, new_ref[1].astype(BF16), preferred_element_type=F32)
        carry_ref[...] = carry

    grp = PAGE_GROUP
    no_mask = [None] * grp
    state = {"carry": carry_ref[...], "pv": None}

    def qk(c, _):
        z = jnp.dot(qb_ref[...], pages(k_refs[c * grp:(c + 1) * grp]), preferred_element_type=F32)
        return z + bias_rows_ref[...]

    def front(c, z):
        return _sb_front([z[:, r * PAGE:(r + 1) * PAGE] for r in range(grp)], no_mask, u2_ref[...])

    def back(c, fr):
        ws, state["carry"] = _sb_back(fr, no_mask, state["carry"])
        pv = _qk(jnp.concatenate(ws, axis=1), pages(v_refs[c * grp:(c + 1) * grp]))
        state["pv"] = pv if state["pv"] is None else state["pv"] + pv

    _emit_skewed(npg // grp, [qk, front, back])
    acc_ref[...] += state["pv"]
    carry_ref[...] = state["carry"]

    @pl.when(j == pl.num_programs(1) - 1)
    def _():
        out = jnp.zeros((nq, SB_WIDTH), F32)
        for h in range(HEADS):
            out = out + jnp.where(lane_head[0:nq] == h, acc_ref[h * nq:(h + 1) * nq, :], 0.0)
        o_ref[...] = out


def _sample_attn(page_table, bias, q, k_new, v_new, pool_k, pool_v):
    nseq, nq, _ = q.shape
    npages = page_table.shape[1]
    npg = PAGES_PER_STEP
    nsteps = npages // npg
    rows = HEADS * nq
    seq_spec = pl.BlockSpec((None, nq, SB_WIDTH), lambda s, j, pt: (s, 0, 0))

    def page_spec(r):
        def imap(s, j, pt):
            return (pt[s * npages + (npages - 1 - (j * npg + r))], 0, 0, 0)
        return pl.BlockSpec((None, HEADS, HEAD_DIM, PAGE), imap)

    grid_spec = pltpu.PrefetchScalarGridSpec(
        num_scalar_prefetch=1,
        grid=(nseq, nsteps),
        in_specs=[pl.BlockSpec(memory_space=pltpu.SMEM), seq_spec, seq_spec, seq_spec]
                 + [page_spec(r) for r in range(npg)] * 2
                 + [pl.BlockSpec((2 * PAGE, PAGE), lambda s, j, pt: (0, 0))],
        out_specs=seq_spec,
        scratch_shapes=[pltpu.VMEM((rows, SB_WIDTH), BF16), pltpu.VMEM((rows, 1), F32),
                        pltpu.VMEM((rows, SB_WIDTH), F32), pltpu.VMEM((rows, LANES), F32),
                        pltpu.VMEM((2, PAGE, SB_WIDTH), F32)],
    )
    return pl.pallas_call(
        _sample_attn_kernel,
        grid_spec=grid_spec,
        out_shape=jax.ShapeDtypeStruct((nseq, nq, SB_WIDTH), F32),
        compiler_params=_params("parallel", "arbitrary"),
        name="sample_attn",
    )(page_table.reshape(-1), bias, q, k_new, v_new, *([pool_k] * npg), *([pool_v] * npg),
      _suffix_matrix(PAGE))


def _ln_swish(c, g, b):
    mu = jnp.mean(c, axis=-1, keepdims=True)
    d = c - mu
    var = jnp.mean(d * d, axis=-1, keepdims=True)
    y = d * lax.rsqrt(var + LN_EPS) * g + b
    return y * jax.nn.sigmoid(y)


CONV_ROWS = 32


def _conv_seq_kernel(u_ref, prev_ref, first_ref, w_ref, b_ref, g_ref, beta_ref, o_ref, hist_ref,
                     shift_ref):
    tc = u_ref.shape[0]
    first = pl.program_id(1) == 0
    hist_ref[0:CONV_HALO, :] = jnp.where(first, first_ref[...], prev_ref[...])
    hist_ref[CONV_HALO:CONV_HALO + tc, :] = u_ref[...]
    nshift = shift_ref.shape[1]
    for r in range(1, SUBLANES):
        shift_ref[r - 1] = hist_ref[pl.ds(r, nshift), :]
    base = CONV_HALO - (CONV_KERNEL - 1)
    for r0 in range(0, tc, CONV_ROWS):
        acc = jnp.broadcast_to(b_ref[...], (CONV_ROWS, CONV_CH))
        for k in range(CONV_KERNEL):
            r = (base + k) % SUBLANES
            start = r0 + base + k - r
            if r == 0:
                window = hist_ref[pl.ds(start, CONV_ROWS), :]
            else:
                window = shift_ref[r - 1, pl.ds(start, CONV_ROWS), :]
            acc = acc + w_ref[k:k + 1, :] * window
        o_ref[r0:r0 + CONV_ROWS, :] = _ln_swish(acc, g_ref[...], beta_ref[...]).astype(o_ref.dtype)


def _conv_seq(u, first_hist, w, b, g, beta, nseq, tc):
    rows = u.shape[0]
    nt = rows // (nseq * tc)
    per = tc // CONV_HALO
    vec = _const_spec((1, CONV_CH))
    return pl.pallas_call(
        _conv_seq_kernel,
        grid=(nseq, nt),
        in_specs=[pl.BlockSpec((tc, CONV_CH), lambda s, i: (s * nt + i, 0)),
                  pl.BlockSpec((CONV_HALO, CONV_CH),
                               lambda s, i: (jnp.maximum((s * nt + i) * per - 1, 0), 0)),
                  _const_spec((CONV_HALO, CONV_CH)), _const_spec((CONV_KERNEL, CONV_CH)), vec, vec, vec],
        out_specs=pl.BlockSpec((tc, CONV_CH), lambda s, i: (s * nt + i, 0)),
        out_shape=jax.ShapeDtypeStruct((rows, CONV_CH), BF16),
        scratch_shapes=[pltpu.VMEM((CONV_HALO + tc, CONV_CH), F32),
                        pltpu.VMEM((SUBLANES - 1, CONV_HALO + tc - SUBLANES, CONV_CH), F32)],
        compiler_params=_params("parallel", "parallel"),
        name="conv_seq",
    )(u, u, first_hist, w, b, g, beta)


def _conv_step_kernel(u_ref, st_ref, w_ref, b_ref, g_ref, beta_ref, o_ref, hist_ref):
    nq, grp, _ = u_ref.shape
    nh = st_ref.shape[0]
    hist_ref[0:nh] = st_ref[...]
    hist_ref[nh:nh + nq] = u_ref[...]
    for t in range(nq):
        acc = jnp.broadcast_to(b_ref[...], (grp, CONV_CH))
        for k in range(CONV_KERNEL):
            acc = acc + w_ref[k:k + 1, :] * hist_ref[t + k]
        o_ref[t] = _ln_swish(acc, g_ref[...], beta_ref[...]).astype(o_ref.dtype)


def _conv_step(u, state, w, b, g, beta, group):
    nq, nseq, _ = u.shape
    nh = state.shape[0]
    vec = _const_spec((1, CONV_CH))
    blk = lambda t: pl.BlockSpec((t, group, CONV_CH), lambda i: (0, i, 0))
    return pl.pallas_call(
        _conv_step_kernel,
        grid=(nseq // group,),
        in_specs=[blk(nq), blk(nh), _const_spec((CONV_KERNEL, CONV_CH)), vec, vec, vec],
        out_specs=blk(nq),
        out_shape=jax.ShapeDtypeStruct((nq, nseq, CONV_CH), BF16),
        scratch_shapes=[pltpu.VMEM((nh + nq, group, CONV_CH), F32)],
        compiler_params=_params("parallel"),
        name="conv_step",
    )(u, state, w, b, g, beta)


def _mix_kernel(x_ref, att_ref, c_ref, gpre_ref, wg_ref, wa_ref, wc_ref, wm_ref, g_ref, o_ref):
    x = x_ref[...]
    h = _rms(x, gpre_ref[...]).astype(BF16)
    gate_att = jax.nn.sigmoid(jnp.dot(h, wg_ref[:, 0:D_MODEL], preferred_element_type=F32))
    a = jnp.dot(att_ref[...], wa_ref[...], preferred_element_type=F32)
    merged = gate_att * a
    gate_conv = jax.nn.sigmoid(jnp.dot(h, wg_ref[:, D_MODEL:2 * D_MODEL], preferred_element_type=F32))
    c = jnp.dot(c_ref[...], wc_ref[...], preferred_element_type=F32)
    merged = (merged + gate_conv * c).astype(BF16)
    m = jnp.dot(merged, wm_ref[...], preferred_element_type=F32)
    o_ref[...] = x + _rms(m, g_ref[...])


def _mix(x, att, cact, gpre, wg, wa, wc, wm, g, tm):
    rows = x.shape[0]
    row = lambda width: pl.BlockSpec((tm, width), lambda i: (i, 0))
    return pl.pallas_call(
        _mix_kernel,
        grid=(rows // tm,),
        in_specs=[row(D_MODEL), row(SB_WIDTH), row(CONV_CH), _const_spec((1, D_MODEL)),
                  _const_spec((D_MODEL, 2 * D_MODEL)),
                  _const_spec((SB_WIDTH, D_MODEL)), _const_spec((CONV_CH, D_MODEL)),
                  _const_spec((D_MODEL, D_MODEL)), _const_spec((1, D_MODEL))],
        out_specs=row(D_MODEL),
        out_shape=jax.ShapeDtypeStruct((rows, D_MODEL), F32),
        compiler_params=_params("parallel"),
        name="mix",
    )(x, att, cact, gpre, wg, wa, wc, wm, g)


def _ffn_tail(x, up, g, wout_ref, gpost_ref):
    act = (jax.nn.gelu(g, approximate=True) * up).astype(BF16)
    o = jnp.dot(act, wout_ref[...], preferred_element_type=F32)
    return x + _rms(o, gpost_ref[...])


def _ffn_seq_kernel(x_ref, ginit_ref, gpre_ref, win_ref, dww_ref, dwb_ref, wout_ref, gpost_ref,
                    y_ref, gtail_ref, gbuf_ref):
    tm = x_ref.shape[0]

    @pl.when(pl.program_id(1) == 0)
    def _():
        gbuf_ref[0:FFN_HALO, :] = ginit_ref[...]

    x = x_ref[...]
    h = _rms(x, gpre_ref[...]).astype(BF16)
    up = jnp.dot(h, win_ref[:, 0:D_FF], preferred_element_type=F32)
    gate = jnp.dot(h, win_ref[:, D_FF:2 * D_FF], preferred_element_type=F32)
    gbuf_ref[FFN_HALO:FFN_HALO + tm, :] = gate
    g = (dww_ref[0:1, :] * gbuf_ref[pl.ds(FFN_HALO - 2, tm), :]
         + dww_ref[1:2, :] * gbuf_ref[pl.ds(FFN_HALO - 1, tm), :]
         + dww_ref[2:3, :] * gate + dwb_ref[...])
    y_ref[...] = _ffn_tail(x, up, g, wout_ref, gpost_ref)
    tail = gbuf_ref[pl.ds(tm, FFN_HALO), :]
    gbuf_ref[0:FFN_HALO, :] = tail
    gtail_ref[...] = tail


def _ffn_seq(x, ginit, gpre, win, dww, dwb, wout, gpost, nseq, tm):
    rows = x.shape[0]
    nt = rows // (nseq * tm)
    return pl.pallas_call(
        _ffn_seq_kernel,
        grid=(nseq, nt),
        in_specs=[pl.BlockSpec((tm, D_MODEL), lambda s, i: (s * nt + i, 0)),
                  _const_spec((FFN_HALO, D_FF)), _const_spec((1, D_MODEL)),
                  _const_spec((D_MODEL, 2 * D_FF)), _const_spec((FFN_KERNEL, D_FF)),
                  _const_spec((1, D_FF)), _const_spec((D_FF, D_MODEL)), _const_spec((1, D_MODEL))],
        out_specs=[pl.BlockSpec((tm, D_MODEL), lambda s, i: (s * nt + i, 0)),
                   pl.BlockSpec((FFN_HALO, D_FF), lambda s, i: (s, 0))],
        out_shape=[jax.ShapeDtypeStruct((rows, D_MODEL), F32),
                   jax.ShapeDtypeStruct((nseq * FFN_HALO, D_FF), F32)],
        scratch_shapes=[pltpu.VMEM((FFN_HALO + tm, D_FF), F32)],
        compiler_params=_params("arbitrary", "arbitrary"),
        name="ffn_seq",
    )(x, ginit, gpre, win, dww, dwb, wout, gpost)


def _ffn_step_kernel(x_ref, st_ref, gpre_ref, win_ref, dww_ref, dwb_ref, wout_ref, gpost_ref,
                     y_ref, gnew_ref, gbuf_ref):
    nq, grp, _ = x_ref.shape
    nh = st_ref.shape[0]
    rows = nq * grp
    x = x_ref[...].reshape(rows, D_MODEL)
    h = _rms(x, gpre_ref[...]).astype(BF16)
    up = jnp.dot(h, win_ref[:, 0:D_FF], preferred_element_type=F32)
    gate = jnp.dot(h, win_ref[:, D_FF:2 * D_FF], preferred_element_type=F32)
    gbuf_ref[0:nh * grp, :] = st_ref[...].reshape(nh * grp, D_FF)
    gbuf_ref[nh * grp:nh * grp + rows, :] = gate
    g = (dww_ref[0:1, :] * gbuf_ref[pl.ds((nh - 2) * grp, rows), :]
         + dww_ref[1:2, :] * gbuf_ref[pl.ds((nh - 1) * grp, rows), :]
         + dww_ref[2:3, :] * gate + dwb_ref[...])
    y_ref[...] = _ffn_tail(x, up, g, wout_ref, gpost_ref).reshape(nq, grp, D_MODEL)
    ntail = gnew_ref.shape[0]
    gnew_ref[...] = gbuf_ref[pl.ds((nh + nq - ntail) * grp, ntail * grp), :].reshape(ntail, grp, D_FF)


def _ffn_step(x, state, gpre, win, dww, dwb, wout, gpost, group):
    nq, nseq, _ = x.shape
    nh = state.shape[0]
    blk = lambda t, width: pl.BlockSpec((t, group, width), lambda i: (0, i, 0))
    return pl.pallas_call(
        _ffn_step_kernel,
        grid=(nseq // group,),
        in_specs=[blk(nq, D_MODEL), blk(nh, D_FF), _const_spec((1, D_MODEL)),
                  _const_spec((D_MODEL, 2 * D_FF)), _const_spec((FFN_KERNEL, D_FF)),
                  _const_spec((1, D_FF)), _const_spec((D_FF, D_MODEL)), _const_spec((1, D_MODEL))],
        out_specs=[blk(nq, D_MODEL), blk(nh, D_FF)],
        out_shape=[jax.ShapeDtypeStruct((nq, nseq, D_MODEL), F32),
                   jax.ShapeDtypeStruct((nh, nseq, D_FF), F32)],
        scratch_shapes=[pltpu.VMEM(((nh + nq) * group, D_FF), F32)],
        compiler_params=_params("parallel"),
        name="ffn_step",
    )(x, state, gpre, win, dww, dwb, wout, gpost)


def _layer(xp, xs, pool_k, pool_v, st_conv, st_ffn, page_table, meta, norm_mix_pre, norm_mix_post,
           w_in, sb_bias, conv_dw_w, conv_dw_b, conv_ln_g, conv_ln_b, w_conv_out, w_attn_out,
           w_mix_out, norm_ffn_pre, norm_ffn_post, w_ffn_in, ffn_dw_w, ffn_dw_b, w_ffn_out):
    batch, seq, _ = xp.shape
    nseq, nq, _ = xs.shape
    vec = lambda a: a.reshape(1, -1).astype(F32)
    w_in_b, wa_b, wc_b, wm_b = (w.astype(BF16) for w in (w_in, w_attn_out, w_conv_out, w_mix_out))
    wfi_b, wfo_b = w_ffn_in.astype(BF16), w_ffn_out.astype(BF16)
    bias = sb_bias.reshape(1, HEADS).astype(F32)
    conv_w = (conv_dw_w, vec(conv_dw_b), vec(conv_ln_g), vec(conv_ln_b))
    ffn_w = (vec(norm_ffn_pre), wfi_b, ffn_dw_w, vec(ffn_dw_b), wfo_b, vec(norm_ffn_post))

    x_rows = xp.reshape(batch * seq, D_MODEL)
    head = jnp.concatenate([jnp.zeros((HEAD_PAD, D_MODEL), F32), meta.astype(F32)], axis=0)
    m_rows = jnp.concatenate([head, jnp.swapaxes(xs, 0, 1).reshape(nq * nseq, D_MODEL)], axis=0)
    tmajor = lambda a: a.reshape(nq, nseq, a.shape[-1])
    smajor = lambda a: jnp.swapaxes(tmajor(a), 0, 1)

    w_proj, w_gate = w_in_b[:, :N_PROJ], w_in_b[:, N_PROJ:]
    q_x, k_x, v_x, kb_x, vb_x, u_x = _inproj(x_rows, vec(norm_mix_pre), w_proj, 512)
    q_m, k_m, v_m, kb_m, vb_m, u_m = _inproj(m_rows, vec(norm_mix_pre), w_proj, 128)
    ht = HEAD_TILE

    att_x = _prompt_attn(bias, q_x, kb_x, vb_x, kb_m[:ht], vb_m[:ht], batch, seq)
    att_h = _head_attn(bias, q_m[:ht], kb_m[:ht], vb_m[:ht])
    k_s = smajor(k_m[ht:])
    v_s = smajor(v_m[ht:])
    att_s = _sample_attn(page_table, bias, smajor(q_m[ht:].astype(F32)), k_s, v_s, pool_k, pool_v)
    att_s = jnp.swapaxes(att_s, 0, 1).reshape(nq * nseq, SB_WIDTH).astype(BF16)
    att_m = jnp.concatenate([att_h, att_s], axis=0)

    u_h = u_m[:ht]
    u_s = tmajor(u_m[ht:])
    st_conv_t = jnp.swapaxes(st_conv, 0, 1)
    c_x = _conv_seq(u_x, u_h[ht - CONV_HALO:], *conv_w, nseq=batch, tc=256)
    c_h = _conv_seq(u_h, jnp.zeros((CONV_HALO, CONV_CH), F32), *conv_w, nseq=1, tc=ht)
    c_s = _conv_step(u_s, st_conv_t, *conv_w, group=32)
    c_m = jnp.concatenate([c_h, c_s.reshape(nq * nseq, CONV_CH)], axis=0)

    mix_w = (vec(norm_mix_pre), w_gate, wa_b, wc_b, wm_b, vec(norm_mix_post))
    xm_x = _mix(x_rows, att_x, c_x, *mix_w, tm=512)
    xm_m = _mix(m_rows, att_m, c_m, *mix_w, tm=128)

    _, g_h = _ffn_seq(xm_m[HEAD_PAD:ht], jnp.zeros((FFN_HALO, D_FF), F32), *ffn_w, nseq=1, tm=N_META)
    y_x, g_x = _ffn_seq(xm_x, g_h, *ffn_w, nseq=batch, tm=512)
    y_s, gate_s = _ffn_step(tmajor(xm_m[ht:]), jnp.swapaxes(st_ffn, 0, 1), *ffn_w, group=32)

    k_meta = jnp.broadcast_to(k_m[HEAD_PAD:ht][None], (batch, N_META, SB_WIDTH))
    v_meta = jnp.broadcast_to(v_m[HEAD_PAD:ht][None], (batch, N_META, SB_WIDTH))
    k_p = jnp.concatenate([k_meta, k_x.reshape(batch, seq, SB_WIDTH)], axis=1)
    v_p = jnp.concatenate([v_meta, v_x.reshape(batch, seq, SB_WIDTH)], axis=1)
    total = N_META + seq
    conv_p = u_x.reshape(batch, seq, CONV_CH)[:, seq - (CONV_KERNEL - 1):]
    ffn_p = g_x.reshape(batch, FFN_HALO, D_FF)[:, FFN_HALO - (FFN_KERNEL - 1):]
    conv_s = jnp.swapaxes(jnp.concatenate([st_conv_t, u_s], axis=0)[nq:], 0, 1)
    ffn_s = jnp.swapaxes(gate_s, 0, 1)
    hd = lambda a, n, t: a.reshape(n, t, HEADS, HEAD_DIM)
    return (y_x.reshape(batch, seq, D_MODEL), jnp.swapaxes(y_s, 0, 1),
            hd(k_p, batch, total), hd(v_p, batch, total), conv_p, ffn_p,
            hd(k_s, nseq, nq), hd(v_s, nseq, nq), conv_s, ffn_s)


def kernel(x_prompt, x_sample, cache_k, cache_v, state_conv, state_ffn, page_table, meta_tokens, norm_mix_pre, norm_mix_post, w_in, sb_bias, conv_dw_w, conv_dw_b, conv_ln_g, conv_ln_b, w_conv_out, w_attn_out, w_mix_out, norm_ffn_pre, norm_ffn_post, w_ffn_in, ffn_dw_w, ffn_dw_b, w_ffn_out):
    depth = w_in.shape[0]
    assert depth == 1, "the prompt/sample layer pipeline below is written for a single layer"
    outs = _layer(x_prompt, x_sample,
                  jnp.transpose(cache_k[0], (0, 2, 3, 1)), jnp.transpose(cache_v[0], (0, 2, 3, 1)),
                  state_conv[0], state_ffn[0], page_table, meta_tokens,
                  norm_mix_pre[0], norm_mix_post[0], w_in[0], sb_bias[0], conv_dw_w[0], conv_dw_b[0],
                  conv_ln_g[0], conv_ln_b[0], w_conv_out[0], w_attn_out[0], w_mix_out[0],
                  norm_ffn_pre[0], norm_ffn_post[0], w_ffn_in[0], ffn_dw_w[0], ffn_dw_b[0], w_ffn_out[0])
    y_p, y_s, k_p, v_p, conv_p, ffn_p, k_s, v_s, conv_s, ffn_s = outs
    lead = lambda a: a[None]
    return (y_p, y_s, lead(k_p), lead(v_p), lead(conv_p), lead(ffn_p), lead(k_s), lead(v_s),
            lead(conv_s), lead(ffn_s))
```

```python
import functools

import jax
import jax.numpy as jnp
from jax import lax
from jax.experimental import pallas as pl
from jax.experimental.pallas import tpu as pltpu

F32 = jnp.float32
BF16 = jnp.bfloat16

D_MODEL = 1024
N_META = 16
HEADS = 8
HEAD_DIM = 64
SB_WIDTH = HEADS * HEAD_DIM
CONV_CH = 512
CONV_KERNEL = 31
FFN_KERNEL = 3
D_FF = 2816
PAGE = 128
RMS_EPS = 1e-6
LN_EPS = 1e-5

LANES = 128
SUBLANES = 8
HEAD_TILE = 128
HEAD_PAD = HEAD_TILE - N_META
CONV_HALO = 32
FFN_HALO = 8
VMEM_LIMIT = 56 * 1024 * 1024

ATT_TQ = 512
ATT_TK = 256
PAGES_PER_STEP = 16
PAGE_GROUP = 4


def _params(*sem):
    return pltpu.CompilerParams(dimension_semantics=sem, vmem_limit_bytes=VMEM_LIMIT)


def _const_spec(shape):
    zeros = (0,) * len(shape)
    return pl.BlockSpec(shape, lambda *_: zeros)


def _rms(x, g):
    return x * lax.rsqrt(jnp.mean(x * x, axis=-1, keepdims=True) + RMS_EPS) * g


N_PROJ = 3 * SB_WIDTH + 2 * CONV_CH


def _inproj_kernel(x_ref, g_ref, w_ref, q_ref, k_ref, v_ref, kb_ref, vb_ref, u_ref):
    h = _rms(x_ref[...], g_ref[...]).astype(BF16)

    def proj(lo, hi):
        return jnp.dot(h, w_ref[:, lo:hi], preferred_element_type=F32)

    w = SB_WIDTH
    q_ref[...] = (proj(0, w) * (HEAD_DIM ** -0.5)).astype(BF16)
    k = proj(w, 2 * w)
    v = proj(2 * w, 3 * w)
    k_ref[...] = k
    v_ref[...] = v
    kb_ref[...] = k.astype(BF16)
    vb_ref[...] = v.astype(BF16)
    c0 = 3 * w
    u_ref[...] = proj(c0, c0 + CONV_CH) * jax.nn.sigmoid(proj(c0 + CONV_CH, c0 + 2 * CONV_CH))


def _inproj(x, g, w_bf, tm):
    rows = x.shape[0]
    row = lambda width: pl.BlockSpec((tm, width), lambda i: (i, 0))
    shp = lambda width, dt: jax.ShapeDtypeStruct((rows, width), dt)
    return pl.pallas_call(
        _inproj_kernel,
        grid=(rows // tm,),
        in_specs=[row(D_MODEL), _const_spec((1, D_MODEL)), _const_spec((D_MODEL, N_PROJ))],
        out_specs=[row(SB_WIDTH)] * 5 + [row(CONV_CH)],
        out_shape=[shp(SB_WIDTH, BF16), shp(SB_WIDTH, F32), shp(SB_WIDTH, F32), shp(SB_WIDTH, BF16),
                   shp(SB_WIDTH, BF16), shp(CONV_CH, F32)],
        compiler_params=_params("parallel"),
        name="inproj",
    )(x, g, w_bf)


LOG2E = 1.4426950408889634
ROW_CHUNK = 64


def _suffix_matrix(tk):
    s = lax.broadcasted_iota(jnp.int32, (tk, tk), 0)
    j = lax.broadcasted_iota(jnp.int32, (tk, tk), 1)
    u = -((s > j).astype(BF16))
    return jnp.concatenate([u, u], axis=0)


def _sb_front(zs, masks, u2n):
    terms = [_sb_terms(z, mask) for z, mask in zip(zs, masks)]
    return _sb_suffix(terms, u2n)


def _sb_terms(z_all, mask_all):
    tq = z_all.shape[0]
    ch = min(tq, ROW_CHUNK)
    nls_l, lb_l, parts = [], [], []
    for r0 in range(0, tq, ch):
        z = z_all[r0:r0 + ch]
        nls = jnp.maximum(z, 0.0) + jnp.log(1.0 + jnp.exp2(jnp.abs(z) * (-LOG2E)))
        lb_l.append(z - nls)
        if mask_all is not None:
            nls = jnp.where(_mask_rows(mask_all, r0, r0 + ch), nls, 0.0)
        hi = nls.astype(BF16)
        lo = (nls - hi.astype(F32)).astype(BF16)
        parts.append(jnp.concatenate([hi, lo], axis=1))
        nls_l.append(nls[:, 0:LANES])
    cat = lambda xs: xs[0] if len(xs) == 1 else jnp.concatenate(xs, axis=0)
    return cat(nls_l), cat(lb_l), cat(parts)


def _sb_suffix(terms, u2n):
    parts = [p for _, _, p in terms]
    stacked = parts[0] if len(parts) == 1 else jnp.concatenate(parts, axis=0)
    later_all = jnp.dot(stacked, u2n, preferred_element_type=F32)
    return [n for n, _, _ in terms], [lb for _, lb, _ in terms], later_all


def _sb_back(front, masks, carry):
    nls0_l, lb_l, later_all = front
    tq = lb_l[0].shape[0]
    carries = []
    for n, nls0 in enumerate(nls0_l):
        carries.append(carry)
        total = later_all[n * tq:(n + 1) * tq, 0:LANES] - nls0
        carry = total if carry is None else carry + total
    ws = []
    ch = min(tq, ROW_CHUNK)
    for n, (lb, mask, c_in) in enumerate(zip(lb_l, masks, carries)):
        w_rows = []
        for r0 in range(0, tq, ch):
            later = later_all[n * tq + r0:n * tq + r0 + ch]
            if c_in is not None:
                later = later + c_in[r0:r0 + ch, 0:1]
            w = jnp.exp2((lb[r0:r0 + ch] + later) * LOG2E)
            if mask is not None:
                w = jnp.where(_mask_rows(mask, r0, r0 + ch), w, 0.0)
            w_rows.append(w.astype(BF16))
        ws.append(w_rows[0] if len(w_rows) == 1 else jnp.concatenate(w_rows, axis=0))
    return ws, carry


def _sb_weights(zs, masks, u2n, carry):
    return _sb_back(_sb_front(zs, masks, u2n), masks, carry)


def _emit_skewed(nchains, stages):
    vals = [None] * nchains
    for t in range(nchains + len(stages) - 1):
        for k in reversed(range(len(stages))):
            c = t - k
            if 0 <= c < nchains:
                vals[c] = stages[k](c, vals[c])


def _qk(q, k):
    return lax.dot_general(q, k, (((1,), (1,)), ((), ())), preferred_element_type=F32)


def _head_key_mask():
    col = lax.broadcasted_iota(jnp.int32, (1, HEAD_TILE), 1)
    return col >= HEAD_PAD


def _mask_rows(mask, r0, r1):
    return mask if mask.shape[0] == 1 else mask[r0:r1]


def _prompt_attn_kernel(bias_ref, q_ref, k_ref, v_ref, kh_ref, vh_ref, u2_ref, u2h_ref, o_ref,
                        acc0_ref, acc1_ref, carry0_ref, carry1_ref):
    acc_refs = (acc0_ref, acc1_ref)
    carry_refs = (carry0_ref, carry1_ref)
    hp = pl.program_id(1)
    i = pl.program_id(2)
    tq, tk = ATT_TQ, ATT_TK
    lane = lax.broadcasted_iota(jnp.int32, (1, LANES), 1)
    row = lax.broadcasted_iota(jnp.int32, (tq, tk), 0)
    col = lax.broadcasted_iota(jnp.int32, (tq, tk), 1)
    head_lanes = [(lane // HEAD_DIM) == s for s in range(2)]
    qm = [jnp.where(hl, q_ref[...], jnp.zeros_like(q_ref[...])) for hl in head_lanes]
    bias = [bias_ref[0, 2 * hp + s] for s in range(2)]

    def attend(k_slab, v_slab, masks, u2, first):
        n = len(masks)
        width = k_slab.shape[0] // n
        rows = lambda slab, c: slab[(n - 1 - c // 2) * width:(n - c // 2) * width]
        carry = [None if first else r[...] for r in carry_refs]
        acc = [None if first else r[...] for r in acc_refs]

        def qk(c, _):
            return _qk(qm[c % 2], rows(k_slab, c)) + bias[c % 2]

        def front(c, z):
            return _sb_front([z], [masks[c // 2]], u2)

        def back(c, fr):
            s = c % 2
            (w,), carry[s] = _sb_back(fr, [masks[c // 2]], carry[s])
            pv = jnp.dot(w, rows(v_slab, c), preferred_element_type=F32)
            acc[s] = pv if acc[s] is None else acc[s] + pv

        _emit_skewed(2 * n, [qk, front, back])
        for s in range(2):
            acc_refs[s][...] = acc[s]
            carry_refs[s][...] = carry[s]

    def slab(ref, first_blk, nblk):
        return ref[pl.ds(pl.multiple_of(first_blk * tk, tk), nblk * tk), :]

    nkb = tq // tk
    assert nkb == 2
    diag_masks = [col + d * tk < row for d in reversed(range(nkb))]
    attend(slab(k_ref, nkb * i, nkb), slab(v_ref, nkb * i, nkb), diag_masks, u2_ref[...], True)

    def body(j, _):
        blk = nkb * (i - 1 - j)
        attend(slab(k_ref, blk, nkb), slab(v_ref, blk, nkb), [None] * nkb, u2_ref[...], False)
        return 0

    lax.fori_loop(0, i, body, 0)
    attend(kh_ref[...], vh_ref[...], [_head_key_mask()], u2h_ref[...], False)
    o_ref[...] = jnp.where(head_lanes[0], acc0_ref[...], acc1_ref[...]).astype(o_ref.dtype)


def _prompt_attn(bias, q, kb, vb, kh, vh, batch, seq):
    nq = seq // ATT_TQ
    pair = lambda rows, imap: pl.BlockSpec((rows, LANES), imap)
    return pl.pallas_call(
        _prompt_attn_kernel,
        grid=(batch, HEADS // 2, nq),
        in_specs=[pl.BlockSpec(memory_space=pltpu.SMEM),
                  pair(ATT_TQ, lambda b, hp, i: (b * nq + i, hp)),
                  pair(seq, lambda b, hp, i: (b, hp)),
                  pair(seq, lambda b, hp, i: (b, hp)),
                  pair(HEAD_TILE, lambda b, hp, i: (0, hp)),
                  pair(HEAD_TILE, lambda b, hp, i: (0, hp)),
                  _const_spec((2 * ATT_TK, ATT_TK)),
                  _const_spec((2 * HEAD_TILE, HEAD_TILE))],
        out_specs=pair(ATT_TQ, lambda b, hp, i: (b * nq + i, hp)),
        out_shape=jax.ShapeDtypeStruct((batch * seq, SB_WIDTH), BF16),
        scratch_shapes=[pltpu.VMEM((ATT_TQ, LANES), F32)] * 4,
        compiler_params=_params("parallel", "parallel", "parallel"),
        name="prompt_attn",
    )(bias, q, kb, vb, kh, vh, _suffix_matrix(ATT_TK), _suffix_matrix(HEAD_TILE))


def _head_attn_kernel(bias_ref, q_ref, k_ref, v_ref, u2_ref, o_ref):
    hp = pl.program_id(0)
    t = HEAD_TILE
    lane = lax.broadcasted_iota(jnp.int32, (1, LANES), 1)
    row = lax.broadcasted_iota(jnp.int32, (t, t), 0)
    col = lax.broadcasted_iota(jnp.int32, (t, t), 1)
    mask = (col < row) & (col >= HEAD_PAD)
    outs = []
    for s in range(2):
        head_lanes = (lane // HEAD_DIM) == s
        qm = jnp.where(head_lanes, q_ref[...], jnp.zeros_like(q_ref[...]))
        z = _qk(qm, k_ref[...]) + bias_ref[0, 2 * hp + s]
        (w,), _ = _sb_weights([z], [mask], u2_ref[...], None)
        outs.append((head_lanes, jnp.dot(w, v_ref[...], preferred_element_type=F32)))
    o_ref[...] = jnp.where(outs[0][0], outs[0][1], outs[1][1]).astype(o_ref.dtype)


def _head_attn(bias, q, kb, vb):
    pair = pl.BlockSpec((HEAD_TILE, LANES), lambda hp: (0, hp))
    return pl.pallas_call(
        _head_attn_kernel,
        grid=(HEADS // 2,),
        in_specs=[pl.BlockSpec(memory_space=pltpu.SMEM), pair, pair, pair,
                  _const_spec((2 * HEAD_TILE, HEAD_TILE))],
        out_specs=pair,
        out_shape=jax.ShapeDtypeStruct((HEAD_TILE, SB_WIDTH), BF16),
        compiler_params=_params("parallel"),
        name="head_attn",
    )(bias, q, kb, vb, _suffix_matrix(HEAD_TILE))


def _sample_attn_kernel(pt_ref, bias_ref, q_ref, kn_ref, vn_ref, *rest, nsteps, nblocks):
    npg = PAGES_PER_STEP
    k_refs = rest[:npg]
    v_refs = rest[npg:2 * npg]
    u2_ref, o_ref, qb_ref, bias_rows_ref, acc_ref, carry_ref, w_ref, new_ref = rest[2 * npg:]
    t = pl.program_id(0)
    slot = lambda block: (block // nsteps) % 2
    nq = q_ref.shape[0]
    rows = HEADS * nq
    row_head = lax.broadcasted_iota(jnp.int32, (rows, SB_WIDTH), 0) // nq
    lane_head = lax.broadcasted_iota(jnp.int32, (rows, SB_WIDTH), 1) // HEAD_DIM

    def pages(refs):
        return jnp.concatenate([r[...].reshape(SB_WIDTH, PAGE).astype(BF16) for r in refs], axis=1)

    @pl.when(t == 0)
    def _():
        w_ref[...] = jnp.zeros_like(w_ref)

    @pl.when((t > nsteps) & ((t - 1) % nsteps == 0))
    def _():
        full = acc_ref[slot(t - 2)]
        out = jnp.zeros((nq, SB_WIDTH), F32)
        for h in range(HEADS):
            out = out + jnp.where(lane_head[0:nq] == h, full[h * nq:(h + 1) * nq, :], 0.0)
        o_ref[...] = out

    @pl.when((t % nsteps == 0) & (t < nblocks))
    def _():
        q = q_ref[...].astype(BF16)
        qrep = jnp.concatenate([q] * HEADS, axis=0)
        qb_ref[...] = jnp.where(row_head == lane_head, qrep, jnp.zeros_like(qrep))
        bias_rows_ref[...] = jnp.concatenate(
            [jnp.full((nq, 1), bias_ref[0, h], F32) for h in range(HEADS)], axis=0)
        new_ref[...] = jnp.zeros_like(new_ref)
        new_ref[0, 0:nq, :] = kn_ref[...]
        new_ref[1, 0:nq, :] = vn_ref[...]
        qi = lax.broadcasted_iota(jnp.int32, (rows, PAGE), 0) % nq
        ki = lax.broadcasted_iota(jnp.int32, (rows, PAGE), 1)
        z = _qk(qb_ref[...], new_ref[0].astype(BF16)) + bias_rows_ref[...]
        (w,), carry = _sb_weights([z], [ki < qi], u2_ref[...], None)
        acc_ref[slot(t)] = jnp.dot(w, new_ref[1].astype(BF16), preferred_element_type=F32)
        carry_ref[...] = carry

    grp = PAGE_GROUP
    w_prev = w_ref[...]
    z = jnp.dot(qb_ref[...], pages(k_refs), preferred_element_type=F32) + bias_rows_ref[...]
    terms = [_sb_terms(z[:, r * PAGE:(r + 1) * PAGE], None) for r in range(npg)]
    front = _sb_suffix(terms, u2_ref[...])
    prev = slot(jnp.maximum(t - 1, 0))
    acc_ref[prev] = acc_ref[prev] + _qk(w_prev, pages(v_refs))
    ws, carry = _sb_back(front, [None] * npg, carry_ref[...])
    w_ref[...] = jnp.concatenate(ws, axis=1)
    carry_ref[...] = carry


def _sample_attn(page_table, bias, q, k_new, v_new, pool_k, pool_v):
    nseq, nq, _ = q.shape
    npages = page_table.shape[1]
    npg = PAGES_PER_STEP
    nsteps = npages // npg
    rows = HEADS * nq
    nblocks = nseq * nsteps
    assert nsteps >= 2
    block = lambda t, lag: jnp.clip(t - lag, 0, nblocks - 1)

    def seq_spec(lag):
        return pl.BlockSpec((None, nq, SB_WIDTH), lambda t, pt: (block(t, lag) // nsteps, 0, 0))

    def page_spec(r, lag):
        def imap(t, pt):
            b = block(t, lag)
            s, j = b // nsteps, b % nsteps
            return (pt[s * npages + (npages - 1 - (j * npg + r))], 0, 0, 0)
        return pl.BlockSpec((None, HEADS, HEAD_DIM, PAGE), imap)

    grid_spec = pltpu.PrefetchScalarGridSpec(
        num_scalar_prefetch=1,
        grid=(nblocks + 2,),
        in_specs=[pl.BlockSpec(memory_space=pltpu.SMEM), seq_spec(0), seq_spec(0), seq_spec(0)]
                 + [page_spec(r, 0) for r in range(npg)] + [page_spec(r, 1) for r in range(npg)]
                 + [pl.BlockSpec((2 * PAGE, PAGE), lambda t, pt: (0, 0))],
        out_specs=seq_spec(2),
        scratch_shapes=[pltpu.VMEM((rows, SB_WIDTH), BF16), pltpu.VMEM((rows, 1), F32),
                        pltpu.VMEM((2, rows, SB_WIDTH), F32), pltpu.VMEM((rows, LANES), F32),
                        pltpu.VMEM((rows, npg * PAGE), BF16), pltpu.VMEM((2, PAGE, SB_WIDTH), F32)],
    )
    return pl.pallas_call(
        functools.partial(_sample_attn_kernel, nsteps=nsteps, nblocks=nblocks),
        grid_spec=grid_spec,
        out_shape=jax.ShapeDtypeStruct((nseq, nq, SB_WIDTH), F32),
        compiler_params=_params("arbitrary"),
        name="sample_attn",
    )(page_table.reshape(-1), bias, q, k_new, v_new, *([pool_k] * npg), *([pool_v] * npg),
      _suffix_matrix(PAGE))


def _ln_swish(c, g, b):
    mu = jnp.mean(c, axis=-1, keepdims=True)
    d = c - mu
    var = jnp.mean(d * d, axis=-1, keepdims=True)
    y = d * lax.rsqrt(var + LN_EPS) * g + b
    return y * jax.nn.sigmoid(y)


CONV_ROWS = 32


def _conv_seq_kernel(u_ref, prev_ref, first_ref, w_ref, b_ref, g_ref, beta_ref, o_ref, hist_ref,
                     shift_ref):
    tc = u_ref.shape[0]
    first = pl.program_id(1) == 0
    hist_ref[0:CONV_HALO, :] = jnp.where(first, first_ref[...], prev_ref[...])
    hist_ref[CONV_HALO:CONV_HALO + tc, :] = u_ref[...]
    nshift = shift_ref.shape[1]
    for r in range(1, SUBLANES):
        shift_ref[r - 1] = hist_ref[pl.ds(r, nshift), :]
    base = CONV_HALO - (CONV_KERNEL - 1)
    for r0 in range(0, tc, CONV_ROWS):
        acc = jnp.broadcast_to(b_ref[...], (CONV_ROWS, CONV_CH))
        for k in range(CONV_KERNEL):
            r = (base + k) % SUBLANES
            start = r0 + base + k - r
            if r == 0:
                window = hist_ref[pl.ds(start, CONV_ROWS), :]
            else:
                window = shift_ref[r - 1, pl.ds(start, CONV_ROWS), :]
            acc = acc + w_ref[k:k + 1, :] * window
        o_ref[r0:r0 + CONV_ROWS, :] = _ln_swish(acc, g_ref[...], beta_ref[...]).astype(o_ref.dtype)


def _conv_seq(u, first_hist, w, b, g, beta, nseq, tc):
    rows = u.shape[0]
    nt = rows // (nseq * tc)
    per = tc // CONV_HALO
    vec = _const_spec((1, CONV_CH))
    return pl.pallas_call(
        _conv_seq_kernel,
        grid=(nseq, nt),
        in_specs=[pl.BlockSpec((tc, CONV_CH), lambda s, i: (s * nt + i, 0)),
                  pl.BlockSpec((CONV_HALO, CONV_CH),
                               lambda s, i: (jnp.maximum((s * nt + i) * per - 1, 0), 0)),
                  _const_spec((CONV_HALO, CONV_CH)), _const_spec((CONV_KERNEL, CONV_CH)), vec, vec, vec],
        out_specs=pl.BlockSpec((tc, CONV_CH), lambda s, i: (s * nt + i, 0)),
        out_shape=jax.ShapeDtypeStruct((rows, CONV_CH), BF16),
        scratch_shapes=[pltpu.VMEM((CONV_HALO + tc, CONV_CH), F32),
                        pltpu.VMEM((SUBLANES - 1, CONV_HALO + tc - SUBLANES, CONV_CH), F32)],
        compiler_params=_params("parallel", "parallel"),
        name="conv_seq",
    )(u, u, first_hist, w, b, g, beta)


def _conv_step_kernel(u_ref, st_ref, w_ref, b_ref, g_ref, beta_ref, o_ref, hist_ref):
    nq, grp, _ = u_ref.shape
    nh = st_ref.shape[0]
    hist_ref[0:nh] = st_ref[...]
    hist_ref[nh:nh + nq] = u_ref[...]
    for t in range(nq):
        acc = jnp.broadcast_to(b_ref[...], (grp, CONV_CH))
        for k in range(CONV_KERNEL):
            acc = acc + w_ref[k:k + 1, :] * hist_ref[t + k]
        o_ref[t] = _ln_swish(acc, g_ref[...], beta_ref[...]).astype(o_ref.dtype)


def _conv_step(u, state, w, b, g, beta, group):
    nq, nseq, _ = u.shape
    nh = state.shape[0]
    vec = _const_spec((1, CONV_CH))
    blk = lambda t: pl.BlockSpec((t, group, CONV_CH), lambda i: (0, i, 0))
    return pl.pallas_call(
        _conv_step_kernel,
        grid=(nseq // group,),
        in_specs=[blk(nq), blk(nh), _const_spec((CONV_KERNEL, CONV_CH)), vec, vec, vec],
        out_specs=blk(nq),
        out_shape=jax.ShapeDtypeStruct((nq, nseq, CONV_CH), BF16),
        scratch_shapes=[pltpu.VMEM((nh + nq, group, CONV_CH), F32)],
        compiler_params=_params("parallel"),
        name="conv_step",
    )(u, state, w, b, g, beta)


def _mix_kernel(x_ref, att_ref, c_ref, gpre_ref, wg_ref, wa_ref, wc_ref, wm_ref, g_ref, o_ref):
    x = x_ref[...]
    h = _rms(x, gpre_ref[...]).astype(BF16)
    gate_att = jax.nn.sigmoid(jnp.dot(h, wg_ref[:, 0:D_MODEL], preferred_element_type=F32))
    a = jnp.dot(att_ref[...], wa_ref[...], preferred_element_type=F32)
    merged = gate_att * a
    gate_conv = jax.nn.sigmoid(jnp.dot(h, wg_ref[:, D_MODEL:2 * D_MODEL], preferred_element_type=F32))
    c = jnp.dot(c_ref[...], wc_ref[...], preferred_element_type=F32)
    merged = (merged + gate_conv * c).astype(BF16)
    m = jnp.dot(merged, wm_ref[...], preferred_element_type=F32)
    o_ref[...] = x + _rms(m, g_ref[...])


def _mix(x, att, cact, gpre, wg, wa, wc, wm, g, tm):
    rows = x.shape[0]
    row = lambda width: pl.BlockSpec((tm, width), lambda i: (i, 0))
    return pl.pallas_call(
        _mix_kernel,
        grid=(rows // tm,),
        in_specs=[row(D_MODEL), row(SB_WIDTH), row(CONV_CH), _const_spec((1, D_MODEL)),
                  _const_spec((D_MODEL, 2 * D_MODEL)),
                  _const_spec((SB_WIDTH, D_MODEL)), _const_spec((CONV_CH, D_MODEL)),
                  _const_spec((D_MODEL, D_MODEL)), _const_spec((1, D_MODEL))],
        out_specs=row(D_MODEL),
        out_shape=jax.ShapeDtypeStruct((rows, D_MODEL), F32),
        compiler_params=_params("parallel"),
        name="mix",
    )(x, att, cact, gpre, wg, wa, wc, wm, g)


def _ffn_tail(x, up, g, wout_ref, gpost_ref):
    act = (jax.nn.gelu(g, approximate=True) * up).astype(BF16)
    o = jnp.dot(act, wout_ref[...], preferred_element_type=F32)
    return x + _rms(o, gpost_ref[...])


def _ffn_seq_kernel(x_ref, ginit_ref, gpre_ref, win_ref, dww_ref, dwb_ref, wout_ref, gpost_ref,
                    y_ref, gtail_ref, gbuf_ref):
    tm = x_ref.shape[0]

    @pl.when(pl.program_id(1) == 0)
    def _():
        gbuf_ref[0:FFN_HALO, :] = ginit_ref[...]

    x = x_ref[...]
    h = _rms(x, gpre_ref[...]).astype(BF16)
    up = jnp.dot(h, win_ref[:, 0:D_FF], preferred_element_type=F32)
    gate = jnp.dot(h, win_ref[:, D_FF:2 * D_FF], preferred_element_type=F32)
    gbuf_ref[FFN_HALO:FFN_HALO + tm, :] = gate
    g = (dww_ref[0:1, :] * gbuf_ref[pl.ds(FFN_HALO - 2, tm), :]
         + dww_ref[1:2, :] * gbuf_ref[pl.ds(FFN_HALO - 1, tm), :]
         + dww_ref[2:3, :] * gate + dwb_ref[...])
    y_ref[...] = _ffn_tail(x, up, g, wout_ref, gpost_ref)
    tail = gbuf_ref[pl.ds(tm, FFN_HALO), :]
    gbuf_ref[0:FFN_HALO, :] = tail
    gtail_ref[...] = tail


def _ffn_seq(x, ginit, gpre, win, dww, dwb, wout, gpost, nseq, tm):
    rows = x.shape[0]
    nt = rows // (nseq * tm)
    return pl.pallas_call(
        _ffn_seq_kernel,
        grid=(nseq, nt),
        in_specs=[pl.BlockSpec((tm, D_MODEL), lambda s, i: (s * nt + i, 0)),
                  _const_spec((FFN_HALO, D_FF)), _const_spec((1, D_MODEL)),
                  _const_spec((D_MODEL, 2 * D_FF)), _const_spec((FFN_KERNEL, D_FF)),
                  _const_spec((1, D_FF)), _const_spec((D_FF, D_MODEL)), _const_spec((1, D_MODEL))],
        out_specs=[pl.BlockSpec((tm, D_MODEL), lambda s, i: (s * nt + i, 0)),
                   pl.BlockSpec((FFN_HALO, D_FF), lambda s, i: (s, 0))],
        out_shape=[jax.ShapeDtypeStruct((rows, D_MODEL), F32),
                   jax.ShapeDtypeStruct((nseq * FFN_HALO, D_FF), F32)],
        scratch_shapes=[pltpu.VMEM((FFN_HALO + tm, D_FF), F32)],
        compiler_params=_params("arbitrary", "arbitrary"),
        name="ffn_seq",
    )(x, ginit, gpre, win, dww, dwb, wout, gpost)


def _ffn_step_kernel(x_ref, st_ref, gpre_ref, win_ref, dww_ref, dwb_ref, wout_ref, gpost_ref,
                     y_ref, gnew_ref, gbuf_ref):
    nq, grp, _ = x_ref.shape
    nh = st_ref.shape[0]
    rows = nq * grp
    x = x_ref[...].reshape(rows, D_MODEL)
    h = _rms(x, gpre_ref[...]).astype(BF16)
    up = jnp.dot(h, win_ref[:, 0:D_FF], preferred_element_type=F32)
    gate = jnp.dot(h, win_ref[:, D_FF:2 * D_FF], preferred_element_type=F32)
    gbuf_ref[0:nh * grp, :] = st_ref[...].reshape(nh * grp, D_FF)
    gbuf_ref[nh * grp:nh * grp + rows, :] = gate
    g = (dww_ref[0:1, :] * gbuf_ref[pl.ds((nh - 2) * grp, rows), :]
         + dww_ref[1:2, :] * gbuf_ref[pl.ds((nh - 1) * grp, rows), :]
         + dww_ref[2:3, :] * gate + dwb_ref[...])
    y_ref[...] = _ffn_tail(x, up, g, wout_ref, gpost_ref).reshape(nq, grp, D_MODEL)
    ntail = gnew_ref.shape[0]
    gnew_ref[...] = gbuf_ref[pl.ds((nh + nq - ntail) * grp, ntail * grp), :].reshape(ntail, grp, D_FF)


def _ffn_step(x, state, gpre, win, dww, dwb, wout, gpost, group):
    nq, nseq, _ = x.shape
    nh = state.shape[0]
    blk = lambda t, width: pl.BlockSpec((t, group, width), lambda i: (0, i, 0))
    return pl.pallas_call(
        _ffn_step_kernel,
        grid=(nseq // group,),
        in_specs=[blk(nq, D_MODEL), blk(nh, D_FF), _const_spec((1, D_MODEL)),
                  _const_spec((D_MODEL, 2 * D_FF)), _const_spec((FFN_KERNEL, D_FF)),
                  _const_spec((1, D_FF)), _const_spec((D_FF, D_MODEL)), _const_spec((1, D_MODEL))],
        out_specs=[blk(nq, D_MODEL), blk(nh, D_FF)],
        out_shape=[jax.ShapeDtypeStruct((nq, nseq, D_MODEL), F32),
                   jax.ShapeDtypeStruct((nh, nseq, D_FF), F32)],
        scratch_shapes=[pltpu.VMEM(((nh + nq) * group, D_FF), F32)],
        compiler_params=_params("parallel"),
        name="ffn_step",
    )(x, state, gpre, win, dww, dwb, wout, gpost)


def _layer(xp, xs, pool_k, pool_v, st_conv, st_ffn, page_table, meta, norm_mix_pre, norm_mix_post,
           w_in, sb_bias, conv_dw_w, conv_dw_b, conv_ln_g, conv_ln_b, w_conv_out, w_attn_out,
           w_mix_out, norm_ffn_pre, norm_ffn_post, w_ffn_in, ffn_dw_w, ffn_dw_b, w_ffn_out):
    batch, seq, _ = xp.shape
    nseq, nq, _ = xs.shape
    vec = lambda a: a.reshape(1, -1).astype(F32)
    w_in_b, wa_b, wc_b, wm_b = (w.astype(BF16) for w in (w_in, w_attn_out, w_conv_out, w_mix_out))
    wfi_b, wfo_b = w_ffn_in.astype(BF16), w_ffn_out.astype(BF16)
    bias = sb_bias.reshape(1, HEADS).astype(F32)
    conv_w = (conv_dw_w, vec(conv_dw_b), vec(conv_ln_g), vec(conv_ln_b))
    ffn_w = (vec(norm_ffn_pre), wfi_b, ffn_dw_w, vec(ffn_dw_b), wfo_b, vec(norm_ffn_post))

    x_rows = xp.reshape(batch * seq, D_MODEL)
    head = jnp.concatenate([jnp.zeros((HEAD_PAD, D_MODEL), F32), meta.astype(F32)], axis=0)
    m_rows = jnp.concatenate([head, jnp.swapaxes(xs, 0, 1).reshape(nq * nseq, D_MODEL)], axis=0)
    tmajor = lambda a: a.reshape(nq, nseq, a.shape[-1])
    smajor = lambda a: jnp.swapaxes(tmajor(a), 0, 1)

    w_proj, w_gate = w_in_b[:, :N_PROJ], w_in_b[:, N_PROJ:]
    q_x, k_x, v_x, kb_x, vb_x, u_x = _inproj(x_rows, vec(norm_mix_pre), w_proj, 512)
    q_m, k_m, v_m, kb_m, vb_m, u_m = _inproj(m_rows, vec(norm_mix_pre), w_proj, 128)
    ht = HEAD_TILE

    att_x = _prompt_attn(bias, q_x, kb_x, vb_x, kb_m[:ht], vb_m[:ht], batch, seq)
    att_h = _head_attn(bias, q_m[:ht], kb_m[:ht], vb_m[:ht])
    k_s = smajor(k_m[ht:])
    v_s = smajor(v_m[ht:])
    att_s = _sample_attn(page_table, bias, smajor(q_m[ht:].astype(F32)), k_s, v_s, pool_k, pool_v)
    att_s = jnp.swapaxes(att_s, 0, 1).reshape(nq * nseq, SB_WIDTH).astype(BF16)
    att_m = jnp.concatenate([att_h, att_s], axis=0)

    u_h = u_m[:ht]
    u_s = tmajor(u_m[ht:])
    st_conv_t = jnp.swapaxes(st_conv, 0, 1)
    c_x = _conv_seq(u_x, u_h[ht - CONV_HALO:], *conv_w, nseq=batch, tc=256)
    c_h = _conv_seq(u_h, jnp.zeros((CONV_HALO, CONV_CH), F32), *conv_w, nseq=1, tc=ht)
    c_s = _conv_step(u_s, st_conv_t, *conv_w, group=32)
    c_m = jnp.concatenate([c_h, c_s.reshape(nq * nseq, CONV_CH)], axis=0)

    mix_w = (vec(norm_mix_pre), w_gate, wa_b, wc_b, wm_b, vec(norm_mix_post))
    xm_x = _mix(x_rows, att_x, c_x, *mix_w, tm=512)
    xm_m = _mix(m_rows, att_m, c_m, *mix_w, tm=128)

    _, g_h = _ffn_seq(xm_m[HEAD_PAD:ht], jnp.zeros((FFN_HALO, D_FF), F32), *ffn_w, nseq=1, tm=N_META)
    y_x, g_x = _ffn_seq(xm_x, g_h, *ffn_w, nseq=batch, tm=512)
    y_s, gate_s = _ffn_step(tmajor(xm_m[ht:]), jnp.swapaxes(st_ffn, 0, 1), *ffn_w, group=32)

    k_meta = jnp.broadcast_to(k_m[HEAD_PAD:ht][None], (batch, N_META, SB_WIDTH))
    v_meta = jnp.broadcast_to(v_m[HEAD_PAD:ht][None], (batch, N_META, SB_WIDTH))
    k_p = jnp.concatenate([k_meta, k_x.reshape(batch, seq, SB_WIDTH)], axis=1)
    v_p = jnp.concatenate([v_meta, v_x.reshape(batch, seq, SB_WIDTH)], axis=1)
    total = N_META + seq
    conv_p = u_x.reshape(batch, seq, CONV_CH)[:, seq - (CONV_KERNEL - 1):]
    ffn_p = g_x.reshape(batch, FFN_HALO, D_FF)[:, FFN_HALO - (FFN_KERNEL - 1):]
    conv_s = jnp.swapaxes(jnp.concatenate([st_conv_t, u_s], axis=0)[nq:], 0, 1)
    ffn_s = jnp.swapaxes(gate_s, 0, 1)
    hd = lambda a, n, t: a.reshape(n, t, HEADS, HEAD_DIM)
    return (y_x.reshape(batch, seq, D_MODEL), jnp.swapaxes(y_s, 0, 1),
            hd(k_p, batch, total), hd(v_p, batch, total), conv_p, ffn_p,
            hd(k_s, nseq, nq), hd(v_s, nseq, nq), conv_s, ffn_s)


def kernel(x_prompt, x_sample, cache_k, cache_v, state_conv, state_ffn, page_table, meta_tokens, norm_mix_pre, norm_mix_post, w_in, sb_bias, conv_dw_w, conv_dw_b, conv_ln_g, conv_ln_b, w_conv_out, w_attn_out, w_mix_out, norm_ffn_pre, norm_ffn_post, w_ffn_in, ffn_dw_w, ffn_dw_b, w_ffn_out):
    depth = w_in.shape[0]
    assert depth == 1, "the prompt/sample layer pipeline below is written for a single layer"
    outs = _layer(x_prompt, x_sample,
                  jnp.transpose(cache_k[0], (0, 2, 3, 1)), jnp.transpose(cache_v[0], (0, 2, 3, 1)),
                  state_conv[0], state_ffn[0], page_table, meta_tokens,
                  norm_mix_pre[0], norm_mix_post[0], w_in[0], sb_bias[0], conv_dw_w[0], conv_dw_b[0],
                  conv_ln_g[0], conv_ln_b[0], w_conv_out[0], w_attn_out[0], w_mix_out[0],
                  norm_ffn_pre[0], norm_ffn_post[0], w_ffn_in[0], ffn_dw_w[0], ffn_dw_b[0], w_ffn_out[0])
    y_p, y_s, k_p, v_p, conv_p, ffn_p, k_s, v_s, conv_s, ffn_s = outs
    lead = lambda a: a[None]
    return (y_p, y_s, lead(k_p), lead(v_p), lead(conv_p), lead(ffn_p), lead(k_s), lead(v_s),
            lead(conv_s), lead(ffn_s))
```

```python
import functools

import jax
import jax.numpy as jnp
from jax import lax
from jax.experimental import pallas as pl
from jax.experimental.pallas import tpu as pltpu

F32 = jnp.float32
BF16 = jnp.bfloat16

D_MODEL = 1024
N_META = 16
HEADS = 8
HEAD_DIM = 64
SB_WIDTH = HEADS * HEAD_DIM
CONV_CH = 512
CONV_KERNEL = 31
FFN_KERNEL = 3
D_FF = 2816
PAGE = 128
RMS_EPS = 1e-6
LN_EPS = 1e-5

LANES = 128
SUBLANES = 8
HEAD_TILE = 128
HEAD_PAD = HEAD_TILE - N_META
CONV_HALO = 32
FFN_HALO = 8
VMEM_LIMIT = 56 * 1024 * 1024

ATT_TQ = 512
ATT_TK = 256
PAGES_PER_STEP = 32


def _params(*sem):
    return pltpu.CompilerParams(dimension_semantics=sem, vmem_limit_bytes=VMEM_LIMIT)


def _const_spec(shape):
    zeros = (0,) * len(shape)
    return pl.BlockSpec(shape, lambda *_: zeros)


def _rms(x, g):
    return x * lax.rsqrt(jnp.mean(x * x, axis=-1, keepdims=True) + RMS_EPS) * g


N_PROJ = 3 * SB_WIDTH + 2 * CONV_CH


def _inproj_kernel(x_ref, g_ref, w_ref, q_ref, k_ref, v_ref, kb_ref, vb_ref, u_ref):
    h = _rms(x_ref[...], g_ref[...]).astype(BF16)

    def proj(lo, hi):
        return jnp.dot(h, w_ref[:, lo:hi], preferred_element_type=F32)

    w = SB_WIDTH
    q_ref[...] = (proj(0, w) * (HEAD_DIM ** -0.5)).astype(BF16)
    k = proj(w, 2 * w)
    v = proj(2 * w, 3 * w)
    k_ref[...] = k
    v_ref[...] = v
    kb_ref[...] = k.astype(BF16)
    vb_ref[...] = v.astype(BF16)
    c0 = 3 * w
    u_ref[...] = proj(c0, c0 + CONV_CH) * jax.nn.sigmoid(proj(c0 + CONV_CH, c0 + 2 * CONV_CH))


def _inproj(x, g, w_bf, tm):
    rows = x.shape[0]
    assert rows % tm == 0, (rows, tm)
    row = lambda width: pl.BlockSpec((tm, width), lambda i: (i, 0))
    shp = lambda width, dt: jax.ShapeDtypeStruct((rows, width), dt)
    return pl.pallas_call(
        _inproj_kernel,
        grid=(rows // tm,),
        in_specs=[row(D_MODEL), _const_spec((1, D_MODEL)), _const_spec((D_MODEL, N_PROJ))],
        out_specs=[row(SB_WIDTH)] * 5 + [row(CONV_CH)],
        out_shape=[shp(SB_WIDTH, BF16), shp(SB_WIDTH, F32), shp(SB_WIDTH, F32), shp(SB_WIDTH, BF16),
                   shp(SB_WIDTH, BF16), shp(CONV_CH, F32)],
        compiler_params=_params("parallel"),
        name="inproj",
    )(x, g, w_bf)


LOG2E = 1.4426950408889634
ROW_CHUNK = 64


def _suffix_matrix(tk):
    s = lax.broadcasted_iota(jnp.int32, (tk, tk), 0)
    j = lax.broadcasted_iota(jnp.int32, (tk, tk), 1)
    u = -((s > j).astype(BF16))
    return jnp.concatenate([u, u], axis=0)


def _sb_front(zs, masks, u2n):
    terms = [_sb_terms(z, mask) for z, mask in zip(zs, masks)]
    return _sb_suffix(terms, u2n)


def _sb_terms(z_all, mask_all):
    tq = z_all.shape[0]
    ch = min(tq, ROW_CHUNK)
    nls_l, lb_l, parts = [], [], []
    for r0 in range(0, tq, ch):
        z = z_all[r0:r0 + ch]
        nls = jnp.maximum(z, 0.0) + jnp.log(1.0 + jnp.exp2(jnp.abs(z) * (-LOG2E)))
        lb_l.append(z - nls)
        if mask_all is not None:
            nls = jnp.where(_mask_rows(mask_all, r0, r0 + ch), nls, 0.0)
        hi = nls.astype(BF16)
        lo = (nls - hi.astype(F32)).astype(BF16)
        parts.append(jnp.concatenate([hi, lo], axis=1))
        nls_l.append(nls[:, 0:LANES])
    cat = lambda xs: xs[0] if len(xs) == 1 else jnp.concatenate(xs, axis=0)
    return cat(nls_l), cat(lb_l), cat(parts)


def _sb_suffix(terms, u2n):
    parts = [p for _, _, p in terms]
    stacked = parts[0] if len(parts) == 1 else jnp.concatenate(parts, axis=0)
    later_all = jnp.dot(stacked, u2n, preferred_element_type=F32)
    return [n for n, _, _ in terms], [lb for _, lb, _ in terms], later_all


def _sb_back(front, masks, carry):
    nls0_l, lb_l, later_all = front
    tq = lb_l[0].shape[0]
    carries = []
    for n, nls0 in enumerate(nls0_l):
        carries.append(carry)
        total = later_all[n * tq:(n + 1) * tq, 0:LANES] - nls0
        carry = total if carry is None else carry + total
    ws = []
    ch = min(tq, ROW_CHUNK)
    for n, (lb, mask, c_in) in enumerate(zip(lb_l, masks, carries)):
        w_rows = []
        for r0 in range(0, tq, ch):
            later = later_all[n * tq + r0:n * tq + r0 + ch]
            if c_in is not None:
                later = later + c_in[r0:r0 + ch, 0:1]
            w = jnp.exp2((lb[r0:r0 + ch] + later) * LOG2E)
            if mask is not None:
                w = jnp.where(_mask_rows(mask, r0, r0 + ch), w, 0.0)
            w_rows.append(w.astype(BF16))
        ws.append(w_rows[0] if len(w_rows) == 1 else jnp.concatenate(w_rows, axis=0))
    return ws, carry


def _sb_weights(zs, masks, u2n, carry):
    return _sb_back(_sb_front(zs, masks, u2n), masks, carry)


def _emit_skewed(nchains, stages):
    vals = [None] * nchains
    for t in range(nchains + len(stages) - 1):
        for k in reversed(range(len(stages))):
            c = t - k
            if 0 <= c < nchains:
                vals[c] = stages[k](c, vals[c])


def _qk(q, k):
    return lax.dot_general(q, k, (((1,), (1,)), ((), ())), preferred_element_type=F32)


def _head_key_mask():
    col = lax.broadcasted_iota(jnp.int32, (1, HEAD_TILE), 1)
    return col >= HEAD_PAD


def _mask_rows(mask, r0, r1):
    return mask if mask.shape[0] == 1 else mask[r0:r1]


def _prompt_attn_kernel(bias_ref, q_ref, k_ref, v_ref, kh_ref, vh_ref, u2_ref, u2h_ref, o_ref,
                        acc0_ref, acc1_ref, carry0_ref, carry1_ref):
    acc_refs = (acc0_ref, acc1_ref)
    carry_refs = (carry0_ref, carry1_ref)
    hp = pl.program_id(1)
    i = pl.program_id(2)
    tq, tk = ATT_TQ, ATT_TK
    lane = lax.broadcasted_iota(jnp.int32, (1, LANES), 1)
    row = lax.broadcasted_iota(jnp.int32, (tq, tk), 0)
    col = lax.broadcasted_iota(jnp.int32, (tq, tk), 1)
    head_lanes = [(lane // HEAD_DIM) == s for s in range(2)]
    qm = [jnp.where(hl, q_ref[...], jnp.zeros_like(q_ref[...])) for hl in head_lanes]
    bias = [bias_ref[0, 2 * hp + s] for s in range(2)]

    def attend(k_slab, v_slab, masks, u2, first):
        n = len(masks)
        width = k_slab.shape[0] // n
        rows = lambda slab, c: slab[(n - 1 - c // 2) * width:(n - c // 2) * width]
        carry = [None if first else r[...] for r in carry_refs]
        acc = [None if first else r[...] for r in acc_refs]

        def qk(c, _):
            return _qk(qm[c % 2], rows(k_slab, c)) + bias[c % 2]

        def front(c, z):
            return _sb_front([z], [masks[c // 2]], u2)

        def back(c, fr):
            s = c % 2
            (w,), carry[s] = _sb_back(fr, [masks[c // 2]], carry[s])
            pv = jnp.dot(w, rows(v_slab, c), preferred_element_type=F32)
            acc[s] = pv if acc[s] is None else acc[s] + pv

        _emit_skewed(2 * n, [qk, front, back])
        for s in range(2):
            acc_refs[s][...] = acc[s]
            carry_refs[s][...] = carry[s]

    def slab(ref, first_blk, nblk):
        return ref[pl.ds(pl.multiple_of(first_blk * tk, tk), nblk * tk), :]

    nkb = tq // tk
    assert nkb == 2
    diag_masks = [col + d * tk < row for d in reversed(range(nkb))]
    attend(slab(k_ref, nkb * i, nkb), slab(v_ref, nkb * i, nkb), diag_masks, u2_ref[...], True)

    def body(j, _):
        blk = nkb * (i - 1 - j)
        attend(slab(k_ref, blk, nkb), slab(v_ref, blk, nkb), [None] * nkb, u2_ref[...], False)
        return 0

    lax.fori_loop(0, i, body, 0)
    attend(kh_ref[...], vh_ref[...], [_head_key_mask()], u2h_ref[...], False)
    o_ref[...] = jnp.where(head_lanes[0], acc0_ref[...], acc1_ref[...]).astype(o_ref.dtype)


def _prompt_attn(bias, q, kb, vb, kh, vh, batch, seq):
    nq = seq // ATT_TQ
    pair = lambda rows, imap: pl.BlockSpec((rows, LANES), imap)
    return pl.pallas_call(
        _prompt_attn_kernel,
        grid=(batch, HEADS // 2, nq),
        in_specs=[pl.BlockSpec(memory_space=pltpu.SMEM),
                  pair(ATT_TQ, lambda b, hp, i: (b * nq + i, hp)),
                  pair(seq, lambda b, hp, i: (b, hp)),
                  pair(seq, lambda b, hp, i: (b, hp)),
                  pair(HEAD_TILE, lambda b, hp, i: (0, hp)),
                  pair(HEAD_TILE, lambda b, hp, i: (0, hp)),
                  _const_spec((2 * ATT_TK, ATT_TK)),
                  _const_spec((2 * HEAD_TILE, HEAD_TILE))],
        out_specs=pair(ATT_TQ, lambda b, hp, i: (b * nq + i, hp)),
        out_shape=jax.ShapeDtypeStruct((batch * seq, SB_WIDTH), BF16),
        scratch_shapes=[pltpu.VMEM((ATT_TQ, LANES), F32)] * 4,
        compiler_params=_params("parallel", "parallel", "parallel"),
        name="prompt_attn",
    )(bias, q, kb, vb, kh, vh, _suffix_matrix(ATT_TK), _suffix_matrix(HEAD_TILE))


def _head_attn_kernel(bias_ref, q_ref, k_ref, v_ref, u2_ref, o_ref):
    hp = pl.program_id(0)
    t = HEAD_TILE
    lane = lax.broadcasted_iota(jnp.int32, (1, LANES), 1)
    row = lax.broadcasted_iota(jnp.int32, (t, t), 0)
    col = lax.broadcasted_iota(jnp.int32, (t, t), 1)
    mask = (col < row) & (col >= HEAD_PAD)
    outs = []
    for s in range(2):
        head_lanes = (lane // HEAD_DIM) == s
        qm = jnp.where(head_lanes, q_ref[...], jnp.zeros_like(q_ref[...]))
        z = _qk(qm, k_ref[...]) + bias_ref[0, 2 * hp + s]
        (w,), _ = _sb_weights([z], [mask], u2_ref[...], None)
        outs.append((head_lanes, jnp.dot(w, v_ref[...], preferred_element_type=F32)))
    o_ref[...] = jnp.where(outs[0][0], outs[0][1], outs[1][1]).astype(o_ref.dtype)


def _head_attn(bias, q, kb, vb):
    pair = pl.BlockSpec((HEAD_TILE, LANES), lambda hp: (0, hp))
    return pl.pallas_call(
        _head_attn_kernel,
        grid=(HEADS // 2,),
        in_specs=[pl.BlockSpec(memory_space=pltpu.SMEM), pair, pair, pair,
                  _const_spec((2 * HEAD_TILE, HEAD_TILE))],
        out_specs=pair,
        out_shape=jax.ShapeDtypeStruct((HEAD_TILE, SB_WIDTH), BF16),
        compiler_params=_params("parallel"),
        name="head_attn",
    )(bias, q, kb, vb, _suffix_matrix(HEAD_TILE))


def _sample_attn_kernel(pt_ref, bias_ref, q_ref, kn_ref, vn_ref, *rest, nsteps, nblocks):
    npg = PAGES_PER_STEP
    k_refs = rest[:npg]
    v_refs = rest[npg:2 * npg]
    u2_ref, o_ref, qb_ref, bias_rows_ref, acc_ref, carry_ref, w_ref, new_ref = rest[2 * npg:]
    t = pl.program_id(0)
    slot = lambda block: (block // nsteps) % 2
    nq = q_ref.shape[0]
    rows = HEADS * nq
    row_head = lax.broadcasted_iota(jnp.int32, (rows, SB_WIDTH), 0) // nq
    lane_head = lax.broadcasted_iota(jnp.int32, (rows, SB_WIDTH), 1) // HEAD_DIM

    def pages(refs):
        return jnp.concatenate([r[...].reshape(SB_WIDTH, PAGE).astype(BF16) for r in refs], axis=1)

    @pl.when(t == 0)
    def _():
        w_ref[...] = jnp.zeros_like(w_ref)

    @pl.when((t > nsteps) & ((t - 1) % nsteps == 0))
    def _():
        full = acc_ref[slot(t - 2)]
        out = jnp.zeros((nq, SB_WIDTH), F32)
        for h in range(HEADS):
            out = out + jnp.where(lane_head[0:nq] == h, full[h * nq:(h + 1) * nq, :], 0.0)
        o_ref[...] = out

    @pl.when((t % nsteps == 0) & (t < nblocks))
    def _():
        q = q_ref[...].astype(BF16)
        qrep = jnp.concatenate([q] * HEADS, axis=0)
        qb_ref[...] = jnp.where(row_head == lane_head, qrep, jnp.zeros_like(qrep))
        bias_rows_ref[...] = jnp.concatenate(
            [jnp.full((nq, 1), bias_ref[0, h], F32) for h in range(HEADS)], axis=0)
        new_ref[...] = jnp.zeros_like(new_ref)
        new_ref[0, 0:nq, :] = kn_ref[...]
        new_ref[1, 0:nq, :] = vn_ref[...]
        qi = lax.broadcasted_iota(jnp.int32, (rows, PAGE), 0) % nq
        ki = lax.broadcasted_iota(jnp.int32, (rows, PAGE), 1)
        z = _qk(qb_ref[...], new_ref[0].astype(BF16)) + bias_rows_ref[...]
        (w,), carry = _sb_weights([z], [ki < qi], u2_ref[...], None)
        acc_ref[slot(t)] = jnp.dot(w, new_ref[1].astype(BF16), preferred_element_type=F32)
        carry_ref[...] = carry

    w_prev = w_ref[...]
    z = jnp.dot(qb_ref[...], pages(k_refs), preferred_element_type=F32) + bias_rows_ref[...]
    terms = [_sb_terms(z[:, r * PAGE:(r + 1) * PAGE], None) for r in range(npg)]
    front = _sb_suffix(terms, u2_ref[...])
    prev = slot(jnp.maximum(t - 1, 0))
    acc_ref[prev] = acc_ref[prev] + _qk(w_prev, pages(v_refs))
    ws, carry = _sb_back(front, [None] * npg, carry_ref[...])
    w_ref[...] = jnp.concatenate(ws, axis=1)
    carry_ref[...] = carry


def _sample_attn(page_table, bias, q, k_new, v_new, pool_k, pool_v):
    nseq, nq, _ = q.shape
    npages = page_table.shape[1]
    npg = PAGES_PER_STEP
    nsteps = npages // npg
    rows = HEADS * nq
    nblocks = nseq * nsteps
    assert nsteps >= 2
    block = lambda t, lag: jnp.clip(t - lag, 0, nblocks - 1)

    def seq_spec(lag):
        return pl.BlockSpec((None, nq, SB_WIDTH), lambda t, pt: (block(t, lag) // nsteps, 0, 0))

    def page_spec(r, lag):
        def imap(t, pt):
            b = block(t, lag)
            s, j = b // nsteps, b % nsteps
            return (pt[s * npages + (npages - 1 - (j * npg + r))], 0, 0, 0)
        return pl.BlockSpec((None, HEADS, HEAD_DIM, PAGE), imap)

    grid_spec = pltpu.PrefetchScalarGridSpec(
        num_scalar_prefetch=1,
        grid=(nblocks + 2,),
        in_specs=[pl.BlockSpec(memory_space=pltpu.SMEM), seq_spec(0), seq_spec(0), seq_spec(0)]
                 + [page_spec(r, 0) for r in range(npg)] + [page_spec(r, 1) for r in range(npg)]
                 + [pl.BlockSpec((2 * PAGE, PAGE), lambda t, pt: (0, 0))],
        out_specs=seq_spec(2),
        scratch_shapes=[pltpu.VMEM((rows, SB_WIDTH), BF16), pltpu.VMEM((rows, 1), F32),
                        pltpu.VMEM((2, rows, SB_WIDTH), F32), pltpu.VMEM((rows, LANES), F32),
                        pltpu.VMEM((rows, npg * PAGE), BF16), pltpu.VMEM((2, PAGE, SB_WIDTH), F32)],
    )
    return pl.pallas_call(
        functools.partial(_sample_attn_kernel, nsteps=nsteps, nblocks=nblocks),
        grid_spec=grid_spec,
        out_shape=jax.ShapeDtypeStruct((nseq, nq, SB_WIDTH), F32),
        compiler_params=_params("arbitrary"),
        name="sample_attn",
    )(page_table.reshape(-1), bias, q, k_new, v_new, *([pool_k] * npg), *([pool_v] * npg),
      _suffix_matrix(PAGE))


def _ln_swish(c, g, b):
    mu = jnp.mean(c, axis=-1, keepdims=True)
    d = c - mu
    var = jnp.mean(d * d, axis=-1, keepdims=True)
    y = d * lax.rsqrt(var + LN_EPS) * g + b
    return y * jax.nn.sigmoid(y)


CONV_ROWS = 32


def _conv_seq_kernel(u_ref, prev_ref, first_ref, w_ref, b_ref, g_ref, beta_ref, o_ref, hist_ref,
                     shift_ref):
    tc = u_ref.shape[0]
    first = pl.program_id(1) == 0
    hist_ref[0:CONV_HALO, :] = jnp.where(first, first_ref[...], prev_ref[...])
    hist_ref[CONV_HALO:CONV_HALO + tc, :] = u_ref[...]
    nshift = shift_ref.shape[1]
    for r in range(1, SUBLANES):
        shift_ref[r - 1] = hist_ref[pl.ds(r, nshift), :]
    base = CONV_HALO - (CONV_KERNEL - 1)
    for r0 in range(0, tc, CONV_ROWS):
        acc = jnp.broadcast_to(b_ref[...], (CONV_ROWS, CONV_CH))
        for k in range(CONV_KERNEL):
            r = (base + k) % SUBLANES
            start = r0 + base + k - r
            if r == 0:
                window = hist_ref[pl.ds(start, CONV_ROWS), :]
            else:
                window = shift_ref[r - 1, pl.ds(start, CONV_ROWS), :]
            acc = acc + w_ref[k:k + 1, :] * window
        o_ref[r0:r0 + CONV_ROWS, :] = _ln_swish(acc, g_ref[...], beta_ref[...]).astype(o_ref.dtype)


def _conv_seq(u, first_hist, w, b, g, beta, nseq, tc):
    rows = u.shape[0]
    assert rows % (nseq * tc) == 0 and tc % CONV_HALO == 0, (rows, nseq, tc)
    nt = rows // (nseq * tc)
    per = tc // CONV_HALO
    vec = _const_spec((1, CONV_CH))
    return pl.pallas_call(
        _conv_seq_kernel,
        grid=(nseq, nt),
        in_specs=[pl.BlockSpec((tc, CONV_CH), lambda s, i: (s * nt + i, 0)),
                  pl.BlockSpec((CONV_HALO, CONV_CH),
                               lambda s, i: (jnp.maximum((s * nt + i) * per - 1, 0), 0)),
                  _const_spec((CONV_HALO, CONV_CH)), _const_spec((CONV_KERNEL, CONV_CH)), vec, vec, vec],
        out_specs=pl.BlockSpec((tc, CONV_CH), lambda s, i: (s * nt + i, 0)),
        out_shape=jax.ShapeDtypeStruct((rows, CONV_CH), BF16),
        scratch_shapes=[pltpu.VMEM((CONV_HALO + tc, CONV_CH), F32),
                        pltpu.VMEM((SUBLANES - 1, CONV_HALO + tc - SUBLANES, CONV_CH), F32)],
        compiler_params=_params("parallel", "parallel"),
        name="conv_seq",
    )(u, u, first_hist, w, b, g, beta)


def _conv_step_kernel(u_ref, st_ref, w_ref, b_ref, g_ref, beta_ref, o_ref, hist_ref):
    nq, grp, _ = u_ref.shape
    nh = st_ref.shape[0]
    hist_ref[0:nh] = st_ref[...]
    hist_ref[nh:nh + nq] = u_ref[...]
    for t in range(nq):
        acc = jnp.broadcast_to(b_ref[...], (grp, CONV_CH))
        for k in range(CONV_KERNEL):
            acc = acc + w_ref[k:k + 1, :] * hist_ref[t + k]
        o_ref[t] = _ln_swish(acc, g_ref[...], beta_ref[...]).astype(o_ref.dtype)


def _conv_step(u, state, w, b, g, beta, group):
    nq, nseq, _ = u.shape
    nh = state.shape[0]
    vec = _const_spec((1, CONV_CH))
    blk = lambda t: pl.BlockSpec((t, group, CONV_CH), lambda i: (0, i, 0))
    return pl.pallas_call(
        _conv_step_kernel,
        grid=(nseq // group,),
        in_specs=[blk(nq), blk(nh), _const_spec((CONV_KERNEL, CONV_CH)), vec, vec, vec],
        out_specs=blk(nq),
        out_shape=jax.ShapeDtypeStruct((nq, nseq, CONV_CH), BF16),
        scratch_shapes=[pltpu.VMEM((nh + nq, group, CONV_CH), F32)],
        compiler_params=_params("parallel"),
        name="conv_step",
    )(u, state, w, b, g, beta)


def _mix_kernel(x_ref, att_ref, c_ref, gpre_ref, wg_ref, wa_ref, wc_ref, wm_ref, g_ref, o_ref):
    x = x_ref[...]
    h = _rms(x, gpre_ref[...]).astype(BF16)
    gate_att = jax.nn.sigmoid(jnp.dot(h, wg_ref[:, 0:D_MODEL], preferred_element_type=F32))
    a = jnp.dot(att_ref[...], wa_ref[...], preferred_element_type=F32)
    merged = gate_att * a
    gate_conv = jax.nn.sigmoid(jnp.dot(h, wg_ref[:, D_MODEL:2 * D_MODEL], preferred_element_type=F32))
    c = jnp.dot(c_ref[...], wc_ref[...], preferred_element_type=F32)
    merged = (merged + gate_conv * c).astype(BF16)
    m = jnp.dot(merged, wm_ref[...], preferred_element_type=F32)
    o_ref[...] = x + _rms(m, g_ref[...])


def _mix(x, att, cact, gpre, wg, wa, wc, wm, g, tm):
    rows = x.shape[0]
    assert rows % tm == 0, (rows, tm)
    row = lambda width: pl.BlockSpec((tm, width), lambda i: (i, 0))
    return pl.pallas_call(
        _mix_kernel,
        grid=(rows // tm,),
        in_specs=[row(D_MODEL), row(SB_WIDTH), row(CONV_CH), _const_spec((1, D_MODEL)),
                  _const_spec((D_MODEL, 2 * D_MODEL)),
                  _const_spec((SB_WIDTH, D_MODEL)), _const_spec((CONV_CH, D_MODEL)),
                  _const_spec((D_MODEL, D_MODEL)), _const_spec((1, D_MODEL))],
        out_specs=row(D_MODEL),
        out_shape=jax.ShapeDtypeStruct((rows, D_MODEL), F32),
        compiler_params=_params("parallel"),
        name="mix",
    )(x, att, cact, gpre, wg, wa, wc, wm, g)


def _ffn_tail(x, up, g, wout_ref, gpost_ref):
    act = (jax.nn.gelu(g, approximate=True) * up).astype(BF16)
    o = jnp.dot(act, wout_ref[...], preferred_element_type=F32)
    return x + _rms(o, gpost_ref[...])


def _ffn_seq_kernel(x_ref, ginit_ref, gpre_ref, win_ref, dww_ref, dwb_ref, wout_ref, gpost_ref,
                    y_ref, gtail_ref, gbuf_ref):
    tm = x_ref.shape[0]

    @pl.when(pl.program_id(1) == 0)
    def _():
        gbuf_ref[0:FFN_HALO, :] = ginit_ref[...]

    x = x_ref[...]
    h = _rms(x, gpre_ref[...]).astype(BF16)
    up = jnp.dot(h, win_ref[:, 0:D_FF], preferred_element_type=F32)
    gate = jnp.dot(h, win_ref[:, D_FF:2 * D_FF], preferred_element_type=F32)
    gbuf_ref[FFN_HALO:FFN_HALO + tm, :] = gate
    g = (dww_ref[0:1, :] * gbuf_ref[pl.ds(FFN_HALO - 2, tm), :]
         + dww_ref[1:2, :] * gbuf_ref[pl.ds(FFN_HALO - 1, tm), :]
         + dww_ref[2:3, :] * gate + dwb_ref[...])
    y_ref[...] = _ffn_tail(x, up, g, wout_ref, gpost_ref)
    tail = gbuf_ref[pl.ds(tm, FFN_HALO), :]
    gbuf_ref[0:FFN_HALO, :] = tail
    gtail_ref[...] = tail


def _ffn_seq(x, ginit, gpre, win, dww, dwb, wout, gpost, nseq, tm):
    rows = x.shape[0]
    assert rows % (nseq * tm) == 0, (rows, nseq, tm)
    nt = rows // (nseq * tm)
    return pl.pallas_call(
        _ffn_seq_kernel,
        grid=(nseq, nt),
        in_specs=[pl.BlockSpec((tm, D_MODEL), lambda s, i: (s * nt + i, 0)),
                  _const_spec((FFN_HALO, D_FF)), _const_spec((1, D_MODEL)),
                  _const_spec((D_MODEL, 2 * D_FF)), _const_spec((FFN_KERNEL, D_FF)),
                  _const_spec((1, D_FF)), _const_spec((D_FF, D_MODEL)), _const_spec((1, D_MODEL))],
        out_specs=[pl.BlockSpec((tm, D_MODEL), lambda s, i: (s * nt + i, 0)),
                   pl.BlockSpec((FFN_HALO, D_FF), lambda s, i: (s, 0))],
        out_shape=[jax.ShapeDtypeStruct((rows, D_MODEL), F32),
                   jax.ShapeDtypeStruct((nseq * FFN_HALO, D_FF), F32)],
        scratch_shapes=[pltpu.VMEM((FFN_HALO + tm, D_FF), F32)],
        compiler_params=_params("arbitrary", "arbitrary"),
        name="ffn_seq",
    )(x, ginit, gpre, win, dww, dwb, wout, gpost)


def _ffn_step_kernel(x_ref, st_ref, gpre_ref, win_ref, dww_ref, dwb_ref, wout_ref, gpost_ref,
                     y_ref, gnew_ref, gbuf_ref):
    nq, grp, _ = x_ref.shape
    nh = st_ref.shape[0]
    rows = nq * grp
    x = x_ref[...].reshape(rows, D_MODEL)
    h = _rms(x, gpre_ref[...]).astype(BF16)
    up = jnp.dot(h, win_ref[:, 0:D_FF], preferred_element_type=F32)
    gate = jnp.dot(h, win_ref[:, D_FF:2 * D_FF], preferred_element_type=F32)
    gbuf_ref[0:nh * grp, :] = st_ref[...].reshape(nh * grp, D_FF)
    gbuf_ref[nh * grp:nh * grp + rows, :] = gate
    g = (dww_ref[0:1, :] * gbuf_ref[pl.ds((nh - 2) * grp, rows), :]
         + dww_ref[1:2, :] * gbuf_ref[pl.ds((nh - 1) * grp, rows), :]
         + dww_ref[2:3, :] * gate + dwb_ref[...])
    y_ref[...] = _ffn_tail(x, up, g, wout_ref, gpost_ref).reshape(nq, grp, D_MODEL)
    ntail = gnew_ref.shape[0]
    gnew_ref[...] = gbuf_ref[pl.ds((nh + nq - ntail) * grp, ntail * grp), :].reshape(ntail, grp, D_FF)


def _ffn_step(x, state, gpre, win, dww, dwb, wout, gpost, group):
    nq, nseq, _ = x.shape
    nh = state.shape[0]
    blk = lambda t, width: pl.BlockSpec((t, group, width), lambda i: (0, i, 0))
    return pl.pallas_call(
        _ffn_step_kernel,
        grid=(nseq // group,),
        in_specs=[blk(nq, D_MODEL), blk(nh, D_FF), _const_spec((1, D_MODEL)),
                  _const_spec((D_MODEL, 2 * D_FF)), _const_spec((FFN_KERNEL, D_FF)),
                  _const_spec((1, D_FF)), _const_spec((D_FF, D_MODEL)), _const_spec((1, D_MODEL))],
        out_specs=[blk(nq, D_MODEL), blk(nh, D_FF)],
        out_shape=[jax.ShapeDtypeStruct((nq, nseq, D_MODEL), F32),
                   jax.ShapeDtypeStruct((nh, nseq, D_FF), F32)],
        scratch_shapes=[pltpu.VMEM(((nh + nq) * group, D_FF), F32)],
        compiler_params=_params("parallel"),
        name="ffn_step",
    )(x, state, gpre, win, dww, dwb, wout, gpost)


def _layer(xp, xs, pool_k, pool_v, st_conv, st_ffn, page_table, meta, norm_mix_pre, norm_mix_post,
           w_in, sb_bias, conv_dw_w, conv_dw_b, conv_ln_g, conv_ln_b, w_conv_out, w_attn_out,
           w_mix_out, norm_ffn_pre, norm_ffn_post, w_ffn_in, ffn_dw_w, ffn_dw_b, w_ffn_out):
    batch, seq, _ = xp.shape
    nseq, nq, _ = xs.shape
    vec = lambda a: a.reshape(1, -1).astype(F32)
    w_in_b, wa_b, wc_b, wm_b = (w.astype(BF16) for w in (w_in, w_attn_out, w_conv_out, w_mix_out))
    wfi_b, wfo_b = w_ffn_in.astype(BF16), w_ffn_out.astype(BF16)
    bias = sb_bias.reshape(1, HEADS).astype(F32)
    conv_w = (conv_dw_w, vec(conv_dw_b), vec(conv_ln_g), vec(conv_ln_b))
    ffn_w = (vec(norm_ffn_pre), wfi_b, ffn_dw_w, vec(ffn_dw_b), wfo_b, vec(norm_ffn_post))

    x_rows = xp.reshape(batch * seq, D_MODEL)
    head = jnp.concatenate([jnp.zeros((HEAD_PAD, D_MODEL), F32), meta.astype(F32)], axis=0)
    m_rows = jnp.concatenate([head, jnp.swapaxes(xs, 0, 1).reshape(nq * nseq, D_MODEL)], axis=0)
    tmajor = lambda a: a.reshape(nq, nseq, a.shape[-1])
    smajor = lambda a: jnp.swapaxes(tmajor(a), 0, 1)

    w_proj, w_gate = w_in_b[:, :N_PROJ], w_in_b[:, N_PROJ:]
    q_x, k_x, v_x, kb_x, vb_x, u_x = _inproj(x_rows, vec(norm_mix_pre), w_proj, 512)
    q_m, k_m, v_m, kb_m, vb_m, u_m = _inproj(m_rows, vec(norm_mix_pre), w_proj, 384)
    ht = HEAD_TILE

    att_x = _prompt_attn(bias, q_x, kb_x, vb_x, kb_m[:ht], vb_m[:ht], batch, seq)
    att_h = _head_attn(bias, q_m[:ht], kb_m[:ht], vb_m[:ht])
    k_s = smajor(k_m[ht:])
    v_s = smajor(v_m[ht:])
    att_s = _sample_attn(page_table, bias, smajor(q_m[ht:].astype(F32)), k_s, v_s, pool_k, pool_v)
    att_s = jnp.swapaxes(att_s, 0, 1).reshape(nq * nseq, SB_WIDTH).astype(BF16)
    att_m = jnp.concatenate([att_h, att_s], axis=0)

    u_h = u_m[:ht]
    u_s = tmajor(u_m[ht:])
    st_conv_t = jnp.swapaxes(st_conv, 0, 1)
    c_x = _conv_seq(u_x, u_h[ht - CONV_HALO:], *conv_w, nseq=batch, tc=256)
    c_h = _conv_seq(u_h, jnp.zeros((CONV_HALO, CONV_CH), F32), *conv_w, nseq=1, tc=ht)
    c_s = _conv_step(u_s, st_conv_t, *conv_w, group=32)
    c_m = jnp.concatenate([c_h, c_s.reshape(nq * nseq, CONV_CH)], axis=0)

    mix_w = (vec(norm_mix_pre), w_gate, wa_b, wc_b, wm_b, vec(norm_mix_post))
    xm_x = _mix(x_rows, att_x, c_x, *mix_w, tm=512)
    xm_m = _mix(m_rows, att_m, c_m, *mix_w, tm=384)

    _, g_h = _ffn_seq(xm_m[HEAD_PAD:ht], jnp.zeros((FFN_HALO, D_FF), F32), *ffn_w, nseq=1, tm=N_META)
    y_x, g_x = _ffn_seq(xm_x, g_h, *ffn_w, nseq=batch, tm=512)
    y_s, gate_s = _ffn_step(tmajor(xm_m[ht:]), jnp.swapaxes(st_ffn, 0, 1), *ffn_w, group=32)

    k_meta = jnp.broadcast_to(k_m[HEAD_PAD:ht][None], (batch, N_META, SB_WIDTH))
    v_meta = jnp.broadcast_to(v_m[HEAD_PAD:ht][None], (batch, N_META, SB_WIDTH))
    k_p = jnp.concatenate([k_meta, k_x.reshape(batch, seq, SB_WIDTH)], axis=1)
    v_p = jnp.concatenate([v_meta, v_x.reshape(batch, seq, SB_WIDTH)], axis=1)
    total = N_META + seq
    conv_p = u_x.reshape(batch, seq, CONV_CH)[:, seq - (CONV_KERNEL - 1):]
    ffn_p = g_x.reshape(batch, FFN_HALO, D_FF)[:, FFN_HALO - (FFN_KERNEL - 1):]
    conv_s = jnp.swapaxes(jnp.concatenate([st_conv_t, u_s], axis=0)[nq:], 0, 1)
    ffn_s = jnp.swapaxes(gate_s, 0, 1)
    hd = lambda a, n, t: a.reshape(n, t, HEADS, HEAD_DIM)
    return (y_x.reshape(batch, seq, D_MODEL), jnp.swapaxes(y_s, 0, 1),
            hd(k_p, batch, total), hd(v_p, batch, total), conv_p, ffn_p,
            hd(k_s, nseq, nq), hd(v_s, nseq, nq), conv_s, ffn_s)


def kernel(x_prompt, x_sample, cache_k, cache_v, state_conv, state_ffn, page_table, meta_tokens, norm_mix_pre, norm_mix_post, w_in, sb_bias, conv_dw_w, conv_dw_b, conv_ln_g, conv_ln_b, w_conv_out, w_attn_out, w_mix_out, norm_ffn_pre, norm_ffn_post, w_ffn_in, ffn_dw_w, ffn_dw_b, w_ffn_out):
    depth = w_in.shape[0]
    assert depth == 1, "the prompt/sample layer pipeline below is written for a single layer"
    outs = _layer(x_prompt, x_sample,
                  jnp.transpose(cache_k[0], (0, 2, 3, 1)), jnp.transpose(cache_v[0], (0, 2, 3, 1)),
                  state_conv[0], state_ffn[0], page_table, meta_tokens,
                  norm_mix_pre[0], norm_mix_post[0], w_in[0], sb_bias[0], conv_dw_w[0], conv_dw_b[0],
                  conv_ln_g[0], conv_ln_b[0], w_conv_out[0], w_attn_out[0], w_mix_out[0],
                  norm_ffn_pre[0], norm_ffn_post[0], w_ffn_in[0], ffn_dw_w[0], ffn_dw_b[0], w_ffn_out[0])
    y_p, y_s, k_p, v_p, conv_p, ffn_p, k_s, v_s, conv_s, ffn_s = outs
    lead = lambda a: a[None]
    return (y_p, y_s, lead(k_p), lead(v_p), lead(conv_p), lead(ffn_p), lead(k_s), lead(v_s),
            lead(conv_s), lead(ffn_s))
```

```python
import functools

import jax
import jax.numpy as jnp
from jax import lax
from jax.experimental import pallas as pl
from jax.experimental.pallas import tpu as pltpu

F32 = jnp.float32
BF16 = jnp.bfloat16

D_MODEL = 1024
N_META = 16
HEADS = 8
HEAD_DIM = 64
SB_WIDTH = HEADS * HEAD_DIM
CONV_CH = 512
CONV_KERNEL = 31
FFN_KERNEL = 3
D_FF = 2816
PAGE = 128
RMS_EPS = 1e-6
LN_EPS = 1e-5

LANES = 128
SUBLANES = 8
HEAD_TILE = 128
HEAD_PAD = HEAD_TILE - N_META
CONV_HALO = 32
FFN_HALO = 8
VMEM_LIMIT = 56 * 1024 * 1024

ATT_TQ = 512
ATT_TK = 256
PAGES_PER_STEP = 32


def _params(*sem):
    return pltpu.CompilerParams(dimension_semantics=sem, vmem_limit_bytes=VMEM_LIMIT)


def _const_spec(shape):
    zeros = (0,) * len(shape)
    return pl.BlockSpec(shape, lambda *_: zeros)


def _rms(x, g):
    return x * lax.rsqrt(jnp.mean(x * x, axis=-1, keepdims=True) + RMS_EPS) * g


N_PROJ = 3 * SB_WIDTH + 2 * CONV_CH


def _inproj_kernel(x_ref, g_ref, w_ref, q_ref, k_ref, v_ref, kb_ref, vb_ref, u_ref):
    h = _rms(x_ref[...], g_ref[...]).astype(BF16)

    def proj(lo, hi):
        return jnp.dot(h, w_ref[:, lo:hi], preferred_element_type=F32)

    w = SB_WIDTH
    q_ref[...] = (proj(0, w) * (HEAD_DIM ** -0.5)).astype(BF16)
    k = proj(w, 2 * w)
    v = proj(2 * w, 3 * w)
    k_ref[...] = k
    v_ref[...] = v
    kb_ref[...] = k.astype(BF16)
    vb_ref[...] = v.astype(BF16)
    c0 = 3 * w
    u_ref[...] = proj(c0, c0 + CONV_CH) * jax.nn.sigmoid(proj(c0 + CONV_CH, c0 + 2 * CONV_CH))


def _inproj(x, g, w_bf, tm):
    rows = x.shape[0]
    assert rows % tm == 0, (rows, tm)
    row = lambda width: pl.BlockSpec((tm, width), lambda i: (i, 0))
    shp = lambda width, dt: jax.ShapeDtypeStruct((rows, width), dt)
    return pl.pallas_call(
        _inproj_kernel,
        grid=(rows // tm,),
        in_specs=[row(D_MODEL), _const_spec((1, D_MODEL)), _const_spec((D_MODEL, N_PROJ))],
        out_specs=[row(SB_WIDTH)] * 5 + [row(CONV_CH)],
        out_shape=[shp(SB_WIDTH, BF16), shp(SB_WIDTH, F32), shp(SB_WIDTH, F32), shp(SB_WIDTH, BF16),
                   shp(SB_WIDTH, BF16), shp(CONV_CH, F32)],
        compiler_params=_params("parallel"),
        name="inproj",
    )(x, g, w_bf)


def _inproj_conv_kernel(x_ref, first_ref, g_ref, w_ref, cw_ref, cb_ref, cg_ref, cbeta_ref,
                        q_ref, k_ref, v_ref, kb_ref, vb_ref, u_ref, c_ref, hist_ref, shift_ref):
    tm = x_ref.shape[0]

    @pl.when(pl.program_id(1) == 0)
    def _():
        hist_ref[0:CONV_HALO, :] = first_ref[...]

    h = _rms(x_ref[...], g_ref[...]).astype(BF16)

    def proj(lo, hi):
        return jnp.dot(h, w_ref[:, lo:hi], preferred_element_type=F32)

    w = SB_WIDTH
    c0 = 3 * w
    u = proj(c0, c0 + CONV_CH) * jax.nn.sigmoid(proj(c0 + CONV_CH, c0 + 2 * CONV_CH))
    u_ref[...] = u
    hist_ref[CONV_HALO:CONV_HALO + tm, :] = u
    _conv_shifts(hist_ref, shift_ref)
    chunks = list(range(0, tm, CONV_ROWS))
    third = -(-len(chunks) // 3)

    def conv(part):
        for r0 in chunks[part * third:(part + 1) * third]:
            _conv_rows(r0, hist_ref, shift_ref, cw_ref, cb_ref, cg_ref, cbeta_ref, c_ref)

    q_ref[...] = (proj(0, w) * (HEAD_DIM ** -0.5)).astype(BF16)
    conv(0)
    k = proj(w, 2 * w)
    k_ref[...] = k
    kb_ref[...] = k.astype(BF16)
    conv(1)
    v = proj(2 * w, 3 * w)
    v_ref[...] = v
    vb_ref[...] = v.astype(BF16)
    conv(2)
    hist_ref[0:CONV_HALO, :] = hist_ref[pl.ds(tm, CONV_HALO), :]


def _inproj_conv(x, first_hist, g, w_bf, cw, cb, cg, cbeta, nseq, tm):
    rows = x.shape[0]
    assert rows % (nseq * tm) == 0 and tm % CONV_ROWS == 0, (rows, nseq, tm)
    nt = rows // (nseq * tm)
    row = lambda width: pl.BlockSpec((tm, width), lambda s, i: (s * nt + i, 0))
    shp = lambda width, dt: jax.ShapeDtypeStruct((rows, width), dt)
    vec = _const_spec((1, CONV_CH))
    return pl.pallas_call(
        _inproj_conv_kernel,
        grid=(nseq, nt),
        in_specs=[row(D_MODEL), _const_spec((CONV_HALO, CONV_CH)), _const_spec((1, D_MODEL)),
                  _const_spec((D_MODEL, N_PROJ)), _const_spec((CONV_KERNEL, CONV_CH)), vec, vec, vec],
        out_specs=[row(SB_WIDTH)] * 5 + [row(CONV_CH), row(CONV_CH)],
        out_shape=[shp(SB_WIDTH, BF16), shp(SB_WIDTH, F32), shp(SB_WIDTH, F32), shp(SB_WIDTH, BF16),
                   shp(SB_WIDTH, BF16), shp(CONV_CH, F32), shp(CONV_CH, BF16)],
        scratch_shapes=[pltpu.VMEM((CONV_HALO + tm, CONV_CH), F32),
                        pltpu.VMEM((SUBLANES - 1, CONV_HALO + tm - SUBLANES, CONV_CH), F32)],
        compiler_params=_params("arbitrary", "arbitrary"),
        name="inproj_conv",
    )(x, first_hist, g, w_bf, cw, cb, cg, cbeta)


LOG2E = 1.4426950408889634
ROW_CHUNK = 64


def _suffix_matrix(tk):
    s = lax.broadcasted_iota(jnp.int32, (tk, tk), 0)
    j = lax.broadcasted_iota(jnp.int32, (tk, tk), 1)
    u = -((s > j).astype(BF16))
    return jnp.concatenate([u, u], axis=0)


def _sb_front(zs, masks, u2n):
    terms = [_sb_terms(z, mask) for z, mask in zip(zs, masks)]
    return _sb_suffix(terms, u2n)


def _sb_terms(z_all, mask_all):
    tq = z_all.shape[0]
    ch = min(tq, ROW_CHUNK)
    nls_l, lb_l, parts = [], [], []
    for r0 in range(0, tq, ch):
        z = z_all[r0:r0 + ch]
        nls = jnp.maximum(z, 0.0) + jnp.log(1.0 + jnp.exp2(jnp.abs(z) * (-LOG2E)))
        lb_l.append(z - nls)
        if mask_all is not None:
            nls = jnp.where(_mask_rows(mask_all, r0, r0 + ch), nls, 0.0)
        hi = nls.astype(BF16)
        lo = (nls - hi.astype(F32)).astype(BF16)
        parts.append(jnp.concatenate([hi, lo], axis=1))
        nls_l.append(nls[:, 0:LANES])
    cat = lambda xs: xs[0] if len(xs) == 1 else jnp.concatenate(xs, axis=0)
    return cat(nls_l), cat(lb_l), cat(parts)


def _sb_suffix(terms, u2n):
    parts = [p for _, _, p in terms]
    stacked = parts[0] if len(parts) == 1 else jnp.concatenate(parts, axis=0)
    later_all = jnp.dot(stacked, u2n, preferred_element_type=F32)
    return [n for n, _, _ in terms], [lb for _, lb, _ in terms], later_all


def _sb_back(front, masks, carry):
    nls0_l, lb_l, later_all = front
    tq = lb_l[0].shape[0]
    carries = []
    for n, nls0 in enumerate(nls0_l):
        carries.append(carry)
        total = later_all[n * tq:(n + 1) * tq, 0:LANES] - nls0
        carry = total if carry is None else carry + total
    ws = []
    ch = min(tq, ROW_CHUNK)
    for n, (lb, mask, c_in) in enumerate(zip(lb_l, masks, carries)):
        w_rows = []
        for r0 in range(0, tq, ch):
            later = later_all[n * tq + r0:n * tq + r0 + ch]
            if c_in is not None:
                later = later + c_in[r0:r0 + ch, 0:1]
            w = jnp.exp2((lb[r0:r0 + ch] + later) * LOG2E)
            if mask is not None:
                w = jnp.where(_mask_rows(mask, r0, r0 + ch), w, 0.0)
            w_rows.append(w.astype(BF16))
        ws.append(w_rows[0] if len(w_rows) == 1 else jnp.concatenate(w_rows, axis=0))
    return ws, carry


def _sb_weights(zs, masks, u2n, carry):
    return _sb_back(_sb_front(zs, masks, u2n), masks, carry)


def _emit_skewed(nchains, stages):
    vals = [None] * nchains
    for t in range(nchains + len(stages) - 1):
        for k in reversed(range(len(stages))):
            c = t - k
            if 0 <= c < nchains:
                vals[c] = stages[k](c, vals[c])


def _qk(q, k):
    return lax.dot_general(q, k, (((1,), (1,)), ((), ())), preferred_element_type=F32)


def _head_key_mask():
    col = lax.broadcasted_iota(jnp.int32, (1, HEAD_TILE), 1)
    return col >= HEAD_PAD


def _mask_rows(mask, r0, r1):
    return mask if mask.shape[0] == 1 else mask[r0:r1]


def _prompt_attn_kernel(bias_ref, q_ref, k_ref, v_ref, kh_ref, vh_ref, u2_ref, u2h_ref, o_ref,
                        acc0_ref, acc1_ref, carry0_ref, carry1_ref):
    acc_refs = (acc0_ref, acc1_ref)
    carry_refs = (carry0_ref, carry1_ref)
    hp = pl.program_id(1)
    i = pl.program_id(2)
    tq, tk = ATT_TQ, ATT_TK
    lane = lax.broadcasted_iota(jnp.int32, (1, LANES), 1)
    row = lax.broadcasted_iota(jnp.int32, (tq, tk), 0)
    col = lax.broadcasted_iota(jnp.int32, (tq, tk), 1)
    head_lanes = [(lane // HEAD_DIM) == s for s in range(2)]
    qm = [jnp.where(hl, q_ref[...], jnp.zeros_like(q_ref[...])) for hl in head_lanes]
    bias = [bias_ref[0, 2 * hp + s] for s in range(2)]

    def attend(k_slab, v_slab, masks, u2, first):
        n = len(masks)
        width = k_slab.shape[0] // n
        rows = lambda slab, c: slab[(n - 1 - c // 2) * width:(n - c // 2) * width]
        carry = [None if first else r[...] for r in carry_refs]
        acc = [None if first else r[...] for r in acc_refs]

        def qk(c, _):
            return _qk(qm[c % 2], rows(k_slab, c)) + bias[c % 2]

        def front(c, z):
            return _sb_front([z], [masks[c // 2]], u2)

        def back(c, fr):
            s = c % 2
            (w,), carry[s] = _sb_back(fr, [masks[c // 2]], carry[s])
            pv = jnp.dot(w, rows(v_slab, c), preferred_element_type=F32)
            acc[s] = pv if acc[s] is None else acc[s] + pv

        _emit_skewed(2 * n, [qk, front, back])
        for s in range(2):
            acc_refs[s][...] = acc[s]
            carry_refs[s][...] = carry[s]

    def slab(ref, first_blk, nblk):
        return ref[pl.ds(pl.multiple_of(first_blk * tk, tk), nblk * tk), :]

    nkb = tq // tk
    assert nkb == 2
    diag_masks = [col + d * tk < row for d in reversed(range(nkb))]
    attend(slab(k_ref, nkb * i, nkb), slab(v_ref, nkb * i, nkb), diag_masks, u2_ref[...], True)

    def body(j, _):
        blk = nkb * (i - 1 - j)
        attend(slab(k_ref, blk, nkb), slab(v_ref, blk, nkb), [None] * nkb, u2_ref[...], False)
        return 0

    lax.fori_loop(0, i, body, 0)
    attend(kh_ref[...], vh_ref[...], [_head_key_mask()], u2h_ref[...], False)
    o_ref[...] = jnp.where(head_lanes[0], acc0_ref[...], acc1_ref[...]).astype(o_ref.dtype)


def _prompt_attn(bias, q, kb, vb, kh, vh, batch, seq):
    nq = seq // ATT_TQ
    pair = lambda rows, imap: pl.BlockSpec((rows, LANES), imap)
    return pl.pallas_call(
        _prompt_attn_kernel,
        grid=(batch, HEADS // 2, nq),
        in_specs=[pl.BlockSpec(memory_space=pltpu.SMEM),
                  pair(ATT_TQ, lambda b, hp, i: (b * nq + i, hp)),
                  pair(seq, lambda b, hp, i: (b, hp)),
                  pair(seq, lambda b, hp, i: (b, hp)),
                  pair(HEAD_TILE, lambda b, hp, i: (0, hp)),
                  pair(HEAD_TILE, lambda b, hp, i: (0, hp)),
                  _const_spec((2 * ATT_TK, ATT_TK)),
                  _const_spec((2 * HEAD_TILE, HEAD_TILE))],
        out_specs=pair(ATT_TQ, lambda b, hp, i: (b * nq + i, hp)),
        out_shape=jax.ShapeDtypeStruct((batch * seq, SB_WIDTH), BF16),
        scratch_shapes=[pltpu.VMEM((ATT_TQ, LANES), F32)] * 4,
        compiler_params=_params("parallel", "parallel", "parallel"),
        name="prompt_attn",
    )(bias, q, kb, vb, kh, vh, _suffix_matrix(ATT_TK), _suffix_matrix(HEAD_TILE))


def _head_attn_kernel(bias_ref, q_ref, k_ref, v_ref, u2_ref, o_ref):
    hp = pl.program_id(0)
    t = HEAD_TILE
    lane = lax.broadcasted_iota(jnp.int32, (1, LANES), 1)
    row = lax.broadcasted_iota(jnp.int32, (t, t), 0)
    col = lax.broadcasted_iota(jnp.int32, (t, t), 1)
    mask = (col < row) & (col >= HEAD_PAD)
    outs = []
    for s in range(2):
        head_lanes = (lane // HEAD_DIM) == s
        qm = jnp.where(head_lanes, q_ref[...], jnp.zeros_like(q_ref[...]))
        z = _qk(qm, k_ref[...]) + bias_ref[0, 2 * hp + s]
        (w,), _ = _sb_weights([z], [mask], u2_ref[...], None)
        outs.append((head_lanes, jnp.dot(w, v_ref[...], preferred_element_type=F32)))
    o_ref[...] = jnp.where(outs[0][0], outs[0][1], outs[1][1]).astype(o_ref.dtype)


def _head_attn(bias, q, kb, vb):
    pair = pl.BlockSpec((HEAD_TILE, LANES), lambda hp: (0, hp))
    return pl.pallas_call(
        _head_attn_kernel,
        grid=(HEADS // 2,),
        in_specs=[pl.BlockSpec(memory_space=pltpu.SMEM), pair, pair, pair,
                  _const_spec((2 * HEAD_TILE, HEAD_TILE))],
        out_specs=pair,
        out_shape=jax.ShapeDtypeStruct((HEAD_TILE, SB_WIDTH), BF16),
        compiler_params=_params("parallel"),
        name="head_attn",
    )(bias, q, kb, vb, _suffix_matrix(HEAD_TILE))


def _sample_attn_kernel(pt_ref, bias_ref, q_ref, kn_ref, vn_ref, *rest, nsteps, nblocks):
    npg = PAGES_PER_STEP
    k_refs = rest[:npg]
    v_refs = rest[npg:2 * npg]
    u2_ref, o_ref, qb_ref, bias_rows_ref, acc_ref, carry_ref, w_ref, new_ref = rest[2 * npg:]
    t = pl.program_id(0)
    slot = lambda block: (block // nsteps) % 2
    nq = q_ref.shape[0]
    rows = HEADS * nq
    row_head = lax.broadcasted_iota(jnp.int32, (rows, SB_WIDTH), 0) // nq
    lane_head = lax.broadcasted_iota(jnp.int32, (rows, SB_WIDTH), 1) // HEAD_DIM

    def pages(refs):
        return jnp.concatenate([r[...].reshape(SB_WIDTH, PAGE).astype(BF16) for r in refs], axis=1)

    @pl.when(t == 0)
    def _():
        w_ref[...] = jnp.zeros_like(w_ref)

    @pl.when((t > nsteps) & ((t - 1) % nsteps == 0))
    def _():
        full = acc_ref[slot(t - 2)]
        out = jnp.zeros((nq, SB_WIDTH), F32)
        for h in range(HEADS):
            out = out + jnp.where(lane_head[0:nq] == h, full[h * nq:(h + 1) * nq, :], 0.0)
        o_ref[...] = out

    @pl.when((t % nsteps == 0) & (t < nblocks))
    def _():
        q = q_ref[...].astype(BF16)
        qrep = jnp.concatenate([q] * HEADS, axis=0)
        qb_ref[...] = jnp.where(row_head == lane_head, qrep, jnp.zeros_like(qrep))
        bias_rows_ref[...] = jnp.concatenate(
            [jnp.full((nq, 1), bias_ref[0, h], F32) for h in range(HEADS)], axis=0)
        new_ref[...] = jnp.zeros_like(new_ref)
        new_ref[0, 0:nq, :] = kn_ref[...]
        new_ref[1, 0:nq, :] = vn_ref[...]
        qi = lax.broadcasted_iota(jnp.int32, (rows, PAGE), 0) % nq
        ki = lax.broadcasted_iota(jnp.int32, (rows, PAGE), 1)
        z = _qk(qb_ref[...], new_ref[0].astype(BF16)) + bias_rows_ref[...]
        (w,), carry = _sb_weights([z], [ki < qi], u2_ref[...], None)
        acc_ref[slot(t)] = jnp.dot(w, new_ref[1].astype(BF16), preferred_element_type=F32)
        carry_ref[...] = carry

    w_prev = w_ref[...]
    z = jnp.dot(qb_ref[...], pages(k_refs), preferred_element_type=F32) + bias_rows_ref[...]
    terms = [_sb_terms(z[:, r * PAGE:(r + 1) * PAGE], None) for r in range(npg)]
    front = _sb_suffix(terms, u2_ref[...])
    prev = slot(jnp.maximum(t - 1, 0))
    acc_ref[prev] = acc_ref[prev] + _qk(w_prev, pages(v_refs))
    ws, carry = _sb_back(front, [None] * npg, carry_ref[...])
    w_ref[...] = jnp.concatenate(ws, axis=1)
    carry_ref[...] = carry


def _sample_attn(page_table, bias, q, k_new, v_new, pool_k, pool_v):
    nseq, nq, _ = q.shape
    npages = page_table.shape[1]
    npg = PAGES_PER_STEP
    nsteps = npages // npg
    rows = HEADS * nq
    nblocks = nseq * nsteps
    assert nsteps >= 2
    block = lambda t, lag: jnp.clip(t - lag, 0, nblocks - 1)

    def seq_spec(lag):
        return pl.BlockSpec((None, nq, SB_WIDTH), lambda t, pt: (block(t, lag) // nsteps, 0, 0))

    def page_spec(r, lag):
        def imap(t, pt):
            b = block(t, lag)
            s, j = b // nsteps, b % nsteps
            return (pt[s * npages + (npages - 1 - (j * npg + r))], 0, 0, 0)
        return pl.BlockSpec((None, HEADS, HEAD_DIM, PAGE), imap)

    grid_spec = pltpu.PrefetchScalarGridSpec(
        num_scalar_prefetch=1,
        grid=(nblocks + 2,),
        in_specs=[pl.BlockSpec(memory_space=pltpu.SMEM), seq_spec(0), seq_spec(0), seq_spec(0)]
                 + [page_spec(r, 0) for r in range(npg)] + [page_spec(r, 1) for r in range(npg)]
                 + [pl.BlockSpec((2 * PAGE, PAGE), lambda t, pt: (0, 0))],
        out_specs=seq_spec(2),
        scratch_shapes=[pltpu.VMEM((rows, SB_WIDTH), BF16), pltpu.VMEM((rows, 1), F32),
                        pltpu.VMEM((2, rows, SB_WIDTH), F32), pltpu.VMEM((rows, LANES), F32),
                        pltpu.VMEM((rows, npg * PAGE), BF16), pltpu.VMEM((2, PAGE, SB_WIDTH), F32)],
    )
    return pl.pallas_call(
        functools.partial(_sample_attn_kernel, nsteps=nsteps, nblocks=nblocks),
        grid_spec=grid_spec,
        out_shape=jax.ShapeDtypeStruct((nseq, nq, SB_WIDTH), F32),
        compiler_params=_params("arbitrary"),
        name="sample_attn",
    )(page_table.reshape(-1), bias, q, k_new, v_new, *([pool_k] * npg), *([pool_v] * npg),
      _suffix_matrix(PAGE))


def _ln_swish(c, g, b):
    mu = jnp.mean(c, axis=-1, keepdims=True)
    d = c - mu
    var = jnp.mean(d * d, axis=-1, keepdims=True)
    y = d * lax.rsqrt(var + LN_EPS) * g + b
    return y * jax.nn.sigmoid(y)


CONV_ROWS = 32


def _conv_shifts(hist_ref, shift_ref):
    for r in range(1, SUBLANES):
        shift_ref[r - 1] = hist_ref[pl.ds(r, shift_ref.shape[1]), :]


def _conv_rows(r0, hist_ref, shift_ref, w_ref, b_ref, g_ref, beta_ref, o_ref):
    base = CONV_HALO - (CONV_KERNEL - 1)
    acc = jnp.broadcast_to(b_ref[...], (CONV_ROWS, CONV_CH))
    for k in range(CONV_KERNEL):
        r = (base + k) % SUBLANES
        start = r0 + base + k - r
        if r == 0:
            window = hist_ref[pl.ds(start, CONV_ROWS), :]
        else:
            window = shift_ref[r - 1, pl.ds(start, CONV_ROWS), :]
        acc = acc + w_ref[k:k + 1, :] * window
    o_ref[r0:r0 + CONV_ROWS, :] = _ln_swish(acc, g_ref[...], beta_ref[...]).astype(o_ref.dtype)


def _conv_seq_kernel(u_ref, prev_ref, first_ref, w_ref, b_ref, g_ref, beta_ref, o_ref, hist_ref,
                     shift_ref):
    tc = u_ref.shape[0]
    first = pl.program_id(1) == 0
    hist_ref[0:CONV_HALO, :] = jnp.where(first, first_ref[...], prev_ref[...])
    hist_ref[CONV_HALO:CONV_HALO + tc, :] = u_ref[...]
    _conv_shifts(hist_ref, shift_ref)
    for r0 in range(0, tc, CONV_ROWS):
        _conv_rows(r0, hist_ref, shift_ref, w_ref, b_ref, g_ref, beta_ref, o_ref)


def _conv_seq(u, first_hist, w, b, g, beta, nseq, tc):
    rows = u.shape[0]
    assert rows % (nseq * tc) == 0 and tc % CONV_HALO == 0, (rows, nseq, tc)
    nt = rows // (nseq * tc)
    per = tc // CONV_HALO
    vec = _const_spec((1, CONV_CH))
    return pl.pallas_call(
        _conv_seq_kernel,
        grid=(nseq, nt),
        in_specs=[pl.BlockSpec((tc, CONV_CH), lambda s, i: (s * nt + i, 0)),
                  pl.BlockSpec((CONV_HALO, CONV_CH),
                               lambda s, i: (jnp.maximum((s * nt + i) * per - 1, 0), 0)),
                  _const_spec((CONV_HALO, CONV_CH)), _const_spec((CONV_KERNEL, CONV_CH)), vec, vec, vec],
        out_specs=pl.BlockSpec((tc, CONV_CH), lambda s, i: (s * nt + i, 0)),
        out_shape=jax.ShapeDtypeStruct((rows, CONV_CH), BF16),
        scratch_shapes=[pltpu.VMEM((CONV_HALO + tc, CONV_CH), F32),
                        pltpu.VMEM((SUBLANES - 1, CONV_HALO + tc - SUBLANES, CONV_CH), F32)],
        compiler_params=_params("parallel", "parallel"),
        name="conv_seq",
    )(u, u, first_hist, w, b, g, beta)


def _conv_step_kernel(u_ref, st_ref, w_ref, b_ref, g_ref, beta_ref, o_ref, hist_ref):
    nq, grp, _ = u_ref.shape
    nh = st_ref.shape[0]
    hist_ref[0:nh] = st_ref[...]
    hist_ref[nh:nh + nq] = u_ref[...]
    for t in range(nq):
        acc = jnp.broadcast_to(b_ref[...], (grp, CONV_CH))
        for k in range(CONV_KERNEL):
            acc = acc + w_ref[k:k + 1, :] * hist_ref[t + k]
        o_ref[t] = _ln_swish(acc, g_ref[...], beta_ref[...]).astype(o_ref.dtype)


def _conv_step(u, state, w, b, g, beta, group):
    nq, nseq, _ = u.shape
    nh = state.shape[0]
    vec = _const_spec((1, CONV_CH))
    blk = lambda t: pl.BlockSpec((t, group, CONV_CH), lambda i: (0, i, 0))
    return pl.pallas_call(
        _conv_step_kernel,
        grid=(nseq // group,),
        in_specs=[blk(nq), blk(nh), _const_spec((CONV_KERNEL, CONV_CH)), vec, vec, vec],
        out_specs=blk(nq),
        out_shape=jax.ShapeDtypeStruct((nq, nseq, CONV_CH), BF16),
        scratch_shapes=[pltpu.VMEM((nh + nq, group, CONV_CH), F32)],
        compiler_params=_params("parallel"),
        name="conv_step",
    )(u, state, w, b, g, beta)


def _mix_kernel(x_ref, att_ref, c_ref, gpre_ref, wg_ref, wa_ref, wc_ref, wm_ref, g_ref, o_ref):
    x = x_ref[...]
    h = _rms(x, gpre_ref[...]).astype(BF16)
    gate_att = jax.nn.sigmoid(jnp.dot(h, wg_ref[:, 0:D_MODEL], preferred_element_type=F32))
    a = jnp.dot(att_ref[...], wa_ref[...], preferred_element_type=F32)
    merged = gate_att * a
    gate_conv = jax.nn.sigmoid(jnp.dot(h, wg_ref[:, D_MODEL:2 * D_MODEL], preferred_element_type=F32))
    c = jnp.dot(c_ref[...], wc_ref[...], preferred_element_type=F32)
    merged = (merged + gate_conv * c).astype(BF16)
    m = jnp.dot(merged, wm_ref[...], preferred_element_type=F32)
    o_ref[...] = x + _rms(m, g_ref[...])


def _mix(x, att, cact, gpre, wg, wa, wc, wm, g, tm):
    rows = x.shape[0]
    assert rows % tm == 0, (rows, tm)
    row = lambda width: pl.BlockSpec((tm, width), lambda i: (i, 0))
    return pl.pallas_call(
        _mix_kernel,
        grid=(rows // tm,),
        in_specs=[row(D_MODEL), row(SB_WIDTH), row(CONV_CH), _const_spec((1, D_MODEL)),
                  _const_spec((D_MODEL, 2 * D_MODEL)),
                  _const_spec((SB_WIDTH, D_MODEL)), _const_spec((CONV_CH, D_MODEL)),
                  _const_spec((D_MODEL, D_MODEL)), _const_spec((1, D_MODEL))],
        out_specs=row(D_MODEL),
        out_shape=jax.ShapeDtypeStruct((rows, D_MODEL), F32),
        compiler_params=_params("parallel"),
        name="mix",
    )(x, att, cact, gpre, wg, wa, wc, wm, g)


def _ffn_tail(x, up, g, wout_ref, gpost_ref):
    act = (jax.nn.gelu(g, approximate=True) * up).astype(BF16)
    o = jnp.dot(act, wout_ref[...], preferred_element_type=F32)
    return x + _rms(o, gpost_ref[...])


def _ffn_seq_kernel(x_ref, ginit_ref, gpre_ref, win_ref, dww_ref, dwb_ref, wout_ref, gpost_ref,
                    y_ref, gtail_ref, gbuf_ref):
    tm = x_ref.shape[0]

    @pl.when(pl.program_id(1) == 0)
    def _():
        gbuf_ref[0:FFN_HALO, :] = ginit_ref[...]

    x = x_ref[...]
    h = _rms(x, gpre_ref[...]).astype(BF16)
    up = jnp.dot(h, win_ref[:, 0:D_FF], preferred_element_type=F32)
    gate = jnp.dot(h, win_ref[:, D_FF:2 * D_FF], preferred_element_type=F32)
    gbuf_ref[FFN_HALO:FFN_HALO + tm, :] = gate
    g = (dww_ref[0:1, :] * gbuf_ref[pl.ds(FFN_HALO - 2, tm), :]
         + dww_ref[1:2, :] * gbuf_ref[pl.ds(FFN_HALO - 1, tm), :]
         + dww_ref[2:3, :] * gate + dwb_ref[...])
    y_ref[...] = _ffn_tail(x, up, g, wout_ref, gpost_ref)
    tail = gbuf_ref[pl.ds(tm, FFN_HALO), :]
    gbuf_ref[0:FFN_HALO, :] = tail
    gtail_ref[...] = tail


def _ffn_seq(x, ginit, gpre, win, dww, dwb, wout, gpost, nseq, tm):
    rows = x.shape[0]
    assert rows % (nseq * tm) == 0, (rows, nseq, tm)
    nt = rows // (nseq * tm)
    return pl.pallas_call(
        _ffn_seq_kernel,
        grid=(nseq, nt),
        in_specs=[pl.BlockSpec((tm, D_MODEL), lambda s, i: (s * nt + i, 0)),
                  _const_spec((FFN_HALO, D_FF)), _const_spec((1, D_MODEL)),
                  _const_spec((D_MODEL, 2 * D_FF)), _const_spec((FFN_KERNEL, D_FF)),
                  _const_spec((1, D_FF)), _const_spec((D_FF, D_MODEL)), _const_spec((1, D_MODEL))],
        out_specs=[pl.BlockSpec((tm, D_MODEL), lambda s, i: (s * nt + i, 0)),
                   pl.BlockSpec((FFN_HALO, D_FF), lambda s, i: (s, 0))],
        out_shape=[jax.ShapeDtypeStruct((rows, D_MODEL), F32),
                   jax.ShapeDtypeStruct((nseq * FFN_HALO, D_FF), F32)],
        scratch_shapes=[pltpu.VMEM((FFN_HALO + tm, D_FF), F32)],
        compiler_params=_params("arbitrary", "arbitrary"),
        name="ffn_seq",
    )(x, ginit, gpre, win, dww, dwb, wout, gpost)


def _ffn_step_kernel(x_ref, st_ref, gpre_ref, win_ref, dww_ref, dwb_ref, wout_ref, gpost_ref,
                     y_ref, gnew_ref, gbuf_ref):
    nq, grp, _ = x_ref.shape
    nh = st_ref.shape[0]
    rows = nq * grp
    x = x_ref[...].reshape(rows, D_MODEL)
    h = _rms(x, gpre_ref[...]).astype(BF16)
    up = jnp.dot(h, win_ref[:, 0:D_FF], preferred_element_type=F32)
    gate = jnp.dot(h, win_ref[:, D_FF:2 * D_FF], preferred_element_type=F32)
    gbuf_ref[0:nh * grp, :] = st_ref[...].reshape(nh * grp, D_FF)
    gbuf_ref[nh * grp:nh * grp + rows, :] = gate
    g = (dww_ref[0:1, :] * gbuf_ref[pl.ds((nh - 2) * grp, rows), :]
         + dww_ref[1:2, :] * gbuf_ref[pl.ds((nh - 1) * grp, rows), :]
         + dww_ref[2:3, :] * gate + dwb_ref[...])
    y_ref[...] = _ffn_tail(x, up, g, wout_ref, gpost_ref).reshape(nq, grp, D_MODEL)
    ntail = gnew_ref.shape[0]
    gnew_ref[...] = gbuf_ref[pl.ds((nh + nq - ntail) * grp, ntail * grp), :].reshape(ntail, grp, D_FF)


def _ffn_step(x, state, gpre, win, dww, dwb, wout, gpost, group):
    nq, nseq, _ = x.shape
    nh = state.shape[0]
    blk = lambda t, width: pl.BlockSpec((t, group, width), lambda i: (0, i, 0))
    return pl.pallas_call(
        _ffn_step_kernel,
        grid=(nseq // group,),
        in_specs=[blk(nq, D_MODEL), blk(nh, D_FF), _const_spec((1, D_MODEL)),
                  _const_spec((D_MODEL, 2 * D_FF)), _const_spec((FFN_KERNEL, D_FF)),
                  _const_spec((1, D_FF)), _const_spec((D_FF, D_MODEL)), _const_spec((1, D_MODEL))],
        out_specs=[blk(nq, D_MODEL), blk(nh, D_FF)],
        out_shape=[jax.ShapeDtypeStruct((nq, nseq, D_MODEL), F32),
                   jax.ShapeDtypeStruct((nh, nseq, D_FF), F32)],
        scratch_shapes=[pltpu.VMEM(((nh + nq) * group, D_FF), F32)],
        compiler_params=_params("parallel"),
        name="ffn_step",
    )(x, state, gpre, win, dww, dwb, wout, gpost)


def _layer(xp, xs, pool_k, pool_v, st_conv, st_ffn, page_table, meta, norm_mix_pre, norm_mix_post,
           w_in, sb_bias, conv_dw_w, conv_dw_b, conv_ln_g, conv_ln_b, w_conv_out, w_attn_out,
           w_mix_out, norm_ffn_pre, norm_ffn_post, w_ffn_in, ffn_dw_w, ffn_dw_b, w_ffn_out):
    batch, seq, _ = xp.shape
    nseq, nq, _ = xs.shape
    vec = lambda a: a.reshape(1, -1).astype(F32)
    w_in_b, wa_b, wc_b, wm_b = (w.astype(BF16) for w in (w_in, w_attn_out, w_conv_out, w_mix_out))
    wfi_b, wfo_b = w_ffn_in.astype(BF16), w_ffn_out.astype(BF16)
    bias = sb_bias.reshape(1, HEADS).astype(F32)
    conv_w = (conv_dw_w, vec(conv_dw_b), vec(conv_ln_g), vec(conv_ln_b))
    ffn_w = (vec(norm_ffn_pre), wfi_b, ffn_dw_w, vec(ffn_dw_b), wfo_b, vec(norm_ffn_post))

    x_rows = xp.reshape(batch * seq, D_MODEL)
    head = jnp.concatenate([jnp.zeros((HEAD_PAD, D_MODEL), F32), meta.astype(F32)], axis=0)
    m_rows = jnp.concatenate([head, jnp.swapaxes(xs, 0, 1).reshape(nq * nseq, D_MODEL)], axis=0)
    tmajor = lambda a: a.reshape(nq, nseq, a.shape[-1])
    smajor = lambda a: jnp.swapaxes(tmajor(a), 0, 1)

    w_proj, w_gate = w_in_b[:, :N_PROJ], w_in_b[:, N_PROJ:]
    q_m, k_m, v_m, kb_m, vb_m, u_m = _inproj(m_rows, vec(norm_mix_pre), w_proj, 384)
    q_x, k_x, v_x, kb_x, vb_x, u_x, c_x = _inproj_conv(
        x_rows, u_m[HEAD_TILE - CONV_HALO:HEAD_TILE], vec(norm_mix_pre), w_proj, *conv_w,
        nseq=batch, tm=512)
    ht = HEAD_TILE

    att_x = _prompt_attn(bias, q_x, kb_x, vb_x, kb_m[:ht], vb_m[:ht], batch, seq)
    att_h = _head_attn(bias, q_m[:ht], kb_m[:ht], vb_m[:ht])
    k_s = smajor(k_m[ht:])
    v_s = smajor(v_m[ht:])
    att_s = _sample_attn(page_table, bias, smajor(q_m[ht:].astype(F32)), k_s, v_s, pool_k, pool_v)
    att_s = jnp.swapaxes(att_s, 0, 1).reshape(nq * nseq, SB_WIDTH).astype(BF16)
    att_m = jnp.concatenate([att_h, att_s], axis=0)

    u_h = u_m[:ht]
    u_s = tmajor(u_m[ht:])
    st_conv_t = jnp.swapaxes(st_conv, 0, 1)
    c_h = _conv_seq(u_h, jnp.zeros((CONV_HALO, CONV_CH), F32), *conv_w, nseq=1, tc=ht)
    c_s = _conv_step(u_s, st_conv_t, *conv_w, group=32)
    c_m = jnp.concatenate([c_h, c_s.reshape(nq * nseq, CONV_CH)], axis=0)

    mix_w = (vec(norm_mix_pre), w_gate, wa_b, wc_b, wm_b, vec(norm_mix_post))
    xm_x = _mix(x_rows, att_x, c_x, *mix_w, tm=512)
    xm_m = _mix(m_rows, att_m, c_m, *mix_w, tm=384)

    _, g_h = _ffn_seq(xm_m[HEAD_PAD:ht], jnp.zeros((FFN_HALO, D_FF), F32), *ffn_w, nseq=1, tm=N_META)
    y_x, g_x = _ffn_seq(xm_x, g_h, *ffn_w, nseq=batch, tm=512)
    y_s, gate_s = _ffn_step(tmajor(xm_m[ht:]), jnp.swapaxes(st_ffn, 0, 1), *ffn_w, group=32)

    k_meta = jnp.broadcast_to(k_m[HEAD_PAD:ht][None], (batch, N_META, SB_WIDTH))
    v_meta = jnp.broadcast_to(v_m[HEAD_PAD:ht][None], (batch, N_META, SB_WIDTH))
    k_p = jnp.concatenate([k_meta, k_x.reshape(batch, seq, SB_WIDTH)], axis=1)
    v_p = jnp.concatenate([v_meta, v_x.reshape(batch, seq, SB_WIDTH)], axis=1)
    total = N_META + seq
    conv_p = u_x.reshape(batch, seq, CONV_CH)[:, seq - (CONV_KERNEL - 1):]
    ffn_p = g_x.reshape(batch, FFN_HALO, D_FF)[:, FFN_HALO - (FFN_KERNEL - 1):]
    conv_s = jnp.swapaxes(jnp.concatenate([st_conv_t, u_s], axis=0)[nq:], 0, 1)
    ffn_s = jnp.swapaxes(gate_s, 0, 1)
    hd = lambda a, n, t: a.reshape(n, t, HEADS, HEAD_DIM)
    return (y_x.reshape(batch, seq, D_MODEL), jnp.swapaxes(y_s, 0, 1),
            hd(k_p, batch, total), hd(v_p, batch, total), conv_p, ffn_p,
            hd(k_s, nseq, nq), hd(v_s, nseq, nq), conv_s, ffn_s)


def kernel(x_prompt, x_sample, cache_k, cache_v, state_conv, state_ffn, page_table, meta_tokens, norm_mix_pre, norm_mix_post, w_in, sb_bias, conv_dw_w, conv_dw_b, conv_ln_g, conv_ln_b, w_conv_out, w_attn_out, w_mix_out, norm_ffn_pre, norm_ffn_post, w_ffn_in, ffn_dw_w, ffn_dw_b, w_ffn_out):
    depth = w_in.shape[0]
    assert depth == 1, "the prompt/sample layer pipeline below is written for a single layer"
    outs = _layer(x_prompt, x_sample,
                  jnp.transpose(cache_k[0], (0, 2, 3, 1)), jnp.transpose(cache_v[0], (0, 2, 3, 1)),
                  state_conv[0], state_ffn[0], page_table, meta_tokens,
                  norm_mix_pre[0], norm_mix_post[0], w_in[0], sb_bias[0], conv_dw_w[0], conv_dw_b[0],
                  conv_ln_g[0], conv_ln_b[0], w_conv_out[0], w_attn_out[0], w_mix_out[0],
                  norm_ffn_pre[0], norm_ffn_post[0], w_ffn_in[0], ffn_dw_w[0], ffn_dw_b[0], w_ffn_out[0])
    y_p, y_s, k_p, v_p, conv_p, ffn_p, k_s, v_s, conv_s, ffn_s = outs
    lead = lambda a: a[None]
    return (y_p, y_s, lead(k_p), lead(v_p), lead(conv_p), lead(ffn_p), lead(k_s), lead(v_s),
            lead(conv_s), lead(ffn_s))
```
